```python
import jax, jax.numpy as jnp
from jax import lax
import numpy as np

D_MODEL = 2048
BATCH = 16
SEQ = 256
DEPTH = 4
DEC_BATCH = 4
DEC_SEQ = 2048
PAST_LEN = 256

GRID_W = 64
N_EVEN = (DEPTH + 1) // 2
N_ODD = DEPTH // 2
GLA_HEADS = 4
GLA_DK = D_MODEL // 4 // GLA_HEADS
GLA_DV = D_MODEL // 2 // GLA_HEADS
GLA_RANK = 16
GLA_TAU = 16.0
GLA_CHUNK = 64
SG_GROUPS = 4
SG_CH = D_MODEL // 2 // SG_GROUPS
SG_CHUNK = 128
QK_W = GLA_HEADS * GLA_DK
V_W = GLA_HEADS * GLA_DV
SG_W = SG_GROUPS * SG_CH
D_IN_EVEN = 2 * QK_W + 2 * V_W + 2 * GLA_RANK + 2 * SG_W
D_MIX = V_W + SG_W
CONV_K = 31
D_FF = ((8 * D_MODEL // 3 + 255) // 256) * 256
N_MOD = 6
EPS = 1e-6

kernel_name = 'hybrid_gla_sgmlp_conformer_diffusion_step'


def rms_norm(x, g):
    xf = x.astype(jnp.float32)
    y = xf * lax.rsqrt(jnp.mean(xf * xf, axis=-1, keepdims=True) + EPS)
    return (y * g.astype(jnp.float32)).astype(x.dtype)


def layer_norm(x, g, b):
    xf = x.astype(jnp.float32)
    mu = jnp.mean(xf, axis=-1, keepdims=True)
    xc = xf - mu
    var = jnp.mean(xc * xc, axis=-1, keepdims=True)
    y = xc * lax.rsqrt(var + EPS) * g.astype(jnp.float32) + b.astype(jnp.float32)
    return y.astype(x.dtype)


def modulation(cvec, w_mod, b_mod):
    m = jax.nn.silu(cvec) @ w_mod + b_mod
    return jnp.split(m[:, None, :], N_MOD, axis=-1)


def gla_scan(q, k, v, log_a, s0):
    bsz, L, H, _ = q.shape
    dv = v.shape[-1]
    n = L // GLA_CHUNK

    def chunks(t):
        return t.astype(jnp.float32).reshape(bsz, n, GLA_CHUNK, H, t.shape[-1]).transpose(1, 0, 3, 2, 4)

    qc, kc, vc, gc = chunks(q), chunks(k), chunks(v), chunks(log_a)
    b = jnp.cumsum(gc, axis=-2)
    b_last = b[..., -1:, :]
    q_in = qc * jnp.exp(b)
    k_in = kc * jnp.exp(-b)
    k_out = kc * jnp.exp(b_last - b)
    mask = jnp.tril(jnp.ones((GLA_CHUNK, GLA_CHUNK), dtype=bool))
    att = jnp.where(mask, jnp.einsum('nbhid,nbhjd->nbhij', q_in, k_in), 0.0)
    o_intra = jnp.einsum('nbhij,nbhjv->nbhiv', att, vc)

    def step(s, inp):
        q_i, k_o, v_i, bl = inp
        o_inter = jnp.einsum('bhid,bhdv->bhiv', q_i, s)
        s_new = jnp.exp(bl[..., 0, :])[..., None] * s + jnp.einsum('bhjd,bhjv->bhdv', k_o, v_i)
        return s_new, o_inter

    s_fin, o_inter = lax.scan(step, s0.astype(jnp.float32), (q_in, k_out, vc, b_last))
    o = (o_intra + o_inter).transpose(1, 0, 3, 2, 4).reshape(bsz, L, H, dv)
    return o, s_fin


def gla_bidir(q, k, v, la_f, la_b, s0_f, s0_b):
    o_f, s_f = gla_scan(q, k, v, la_f, s0_f)
    rev = lambda t: jnp.flip(t, axis=1)
    o_b, s_b = gla_scan(rev(q), rev(k), rev(v), rev(la_b), s0_b)
    return o_f + rev(o_b), s_f, s_b


def even_mixer(h, w_in, w_gate_up, b_gate, gla_g, sg_g, sg_b, w_sp, b_sp, w_out, s0_f, s0_b):
    bsz, L, _ = h.shape
    z = h @ w_in
    o1 = QK_W
    o2 = 2 * QK_W
    o3 = o2 + V_W
    o4 = o3 + V_W
    o5 = o4 + 2 * GLA_RANK
    q, k, v, r, g_lr, sg = jnp.split(z, [o1, o2, o3, o4, o5], axis=-1)
    q = q.reshape(bsz, L, GLA_HEADS, GLA_DK) * (GLA_DK ** -0.5)
    k = k.reshape(bsz, L, GLA_HEADS, GLA_DK)
    v = v.reshape(bsz, L, GLA_HEADS, GLA_DV)
    g_lr = g_lr.reshape(bsz, L, 2, GLA_RANK)
    logits = jnp.einsum('blzr,zre->blze', g_lr, w_gate_up) + b_gate
    log_a = jax.nn.log_sigmoid(logits.astype(jnp.float32)) / GLA_TAU
    la_f = log_a[:, :, 0].reshape(bsz, L, GLA_HEADS, GLA_DK)
    la_b = log_a[:, :, 1].reshape(bsz, L, GLA_HEADS, GLA_DK)
    o, s_f, s_b = gla_bidir(q, k, v, la_f, la_b, s0_f, s0_b)
    o = rms_norm(o, gla_g).reshape(bsz, L, V_W).astype(h.dtype) * jax.nn.silu(r)
    sg = jax.nn.gelu(sg)
    u, vv = jnp.split(sg, 2, axis=-1)
    vv = layer_norm(vv.reshape(bsz, L, SG_GROUPS, SG_CH), sg_g, sg_b)
    vv = vv.reshape(bsz, L // SG_CHUNK, SG_CHUNK, SG_GROUPS, SG_CH)
    vv = jnp.einsum('gpq,bnqgc->bnpgc', w_sp, vv) + b_sp.T[None, None, :, :, None]
    sg_out = u * vv.reshape(bsz, L, SG_W)
    y = jnp.concatenate([o, sg_out], axis=-1) @ w_out
    return y, s_f, s_b


def dwconv(z, w, b):
    out = lax.conv_general_dilated(
        z, w[:, None, :].astype(z.dtype), window_strides=(1,),
        padding=[(CONV_K // 2, CONV_K // 2)],
        dimension_numbers=('NWC', 'WIO', 'NWC'), feature_group_count=z.shape[-1])
    return out + b


def grid_dwconv(z, w, b, vertical):
    bsz, L, d = z.shape
    rows = L // GRID_W
    g = z.reshape(bsz, rows, GRID_W, d)
    if vertical:
        g = g.transpose(0, 2, 1, 3).reshape(bsz * GRID_W, rows, d)
        g = dwconv(g, w, b).reshape(bsz, GRID_W, rows, d).transpose(0, 2, 1, 3)
    else:
        g = dwconv(g.reshape(bsz * rows, GRID_W, d), w, b)
    return g.reshape(bsz, L, d)


def odd_mixer(h, w_pw1, b_pw1, w_dw, b_dw, cn_g, cn_b, w_pw2, b_pw2, conv_fn):
    a, gt = jnp.split(h @ w_pw1 + b_pw1, 2, axis=-1)
    z = a * jax.nn.sigmoid(gt)
    z = conv_fn(z, w_dw, b_dw)
    z = jax.nn.silu(layer_norm(z, cn_g, cn_b))
    return z @ w_pw2 + b_pw2


def swiglu(h, w_in, w_out):
    a, b = jnp.split(h @ w_in, 2, axis=-1)
    return (jax.nn.silu(a) * b) @ w_out


def setup_inputs(seed: int = 0) -> dict:
    key = jax.random.key(seed)
    ks = iter(jax.random.split(key, 40))
    f32 = jnp.float32

    def nrm(shape, scale=1.0):
        return jax.random.normal(next(ks), shape, f32) * scale

    def gain(shape):
        return 1.0 + 0.02 * jax.random.normal(next(ks), shape, f32)

    D = D_MODEL
    return {
        'x_prompt': nrm((BATCH, SEQ, D)),
        'x_sample': nrm((DEC_BATCH, DEC_SEQ, D)),
        'state_gla': nrm((DEC_BATCH, N_EVEN, 2, GLA_HEADS, GLA_DK, GLA_DV)),
        'c': nrm((DEC_BATCH, D)),
        'c_ctx': nrm((D,)),
        'norm_g': gain((DEPTH, 4, D)),
        'w_mod': nrm((DEPTH, D, N_MOD * D), 0.3 * D ** -0.5),
        'b_mod': nrm((DEPTH, N_MOD * D), 0.02),
        'w_in_a': nrm((N_EVEN, D, D_IN_EVEN), D ** -0.5),
        'w_gate_up': nrm((N_EVEN, 2, GLA_RANK, QK_W), GLA_RANK ** -0.5),
        'b_gate': nrm((N_EVEN, 2, QK_W), 0.02),
        'gla_norm_g': gain((N_EVEN, GLA_HEADS, GLA_DV)),
        'sg_norm_g': gain((N_EVEN, SG_GROUPS, SG_CH)),
        'sg_norm_b': nrm((N_EVEN, SG_GROUPS, SG_CH), 0.02),
        'w_spatial': nrm((N_EVEN, SG_GROUPS, SG_CHUNK, SG_CHUNK), SG_CHUNK ** -0.5),
        'b_spatial': gain((N_EVEN, SG_GROUPS, SG_CHUNK)),
        'w_out_a': nrm((N_EVEN, D_MIX, D), D_MIX ** -0.5),
        'w_pw1': nrm((N_ODD, D, 2 * D), D ** -0.5),
        'b_pw1': nrm((N_ODD, 2 * D), 0.02),
        'w_dw': nrm((N_ODD, CONV_K, D), CONV_K ** -0.5),
        'b_dw': nrm((N_ODD, D), 0.02),
        'conv_norm_g': gain((N_ODD, D)),
        'conv_norm_b': nrm((N_ODD, D), 0.02),
        'w_pw2': nrm((N_ODD, D, D), D ** -0.5),
        'b_pw2': nrm((N_ODD, D), 0.02),
        'w_ffn_in': nrm((DEPTH, D, 2 * D_FF), D ** -0.5),
        'w_ffn_out': nrm((DEPTH, D_FF, D), D_FF ** -0.5),
    }


def reference(x_prompt, x_sample, state_gla, c, c_ctx, norm_g, w_mod, b_mod,
              w_in_a, w_gate_up, b_gate, gla_norm_g, sg_norm_g, sg_norm_b, w_spatial, b_spatial, w_out_a,
              w_pw1, b_pw1, w_dw, b_dw, conv_norm_g, conv_norm_b, w_pw2, b_pw2,
              w_ffn_in, w_ffn_out):
    bp = x_prompt.shape[0]
    xp, xs = x_prompt, x_sample
    new_states = []
    for layer in range(DEPTH):
        mp = modulation(c_ctx[None, :], w_mod[layer], b_mod[layer])
        ms = modulation(c, w_mod[layer], b_mod[layer])
        g = norm_g[layer]
        hp = rms_norm(xp, g[0]) * (1.0 + mp[1]) + mp[0]
        hs = rms_norm(xs, g[0]) * (1.0 + ms[1]) + ms[0]
        i = layer // 2
        if layer % 2 == 0:
            params = (w_in_a[i], w_gate_up[i], b_gate[i], gla_norm_g[i], sg_norm_g[i], sg_norm_b[i],
                      w_spatial[i], b_spatial[i], w_out_a[i])
            zeros = jnp.zeros((bp, GLA_HEADS, GLA_DK, GLA_DV), jnp.float32)
            yp, s_f, s_b = even_mixer(hp, *params, zeros, zeros)
            ys, _, _ = even_mixer(hs, *params, state_gla[:, i, 0], state_gla[:, i, 1])
            new_states.append(jnp.stack([s_f, s_b], axis=1))
        else:
            params = (w_pw1[i], b_pw1[i], w_dw[i], b_dw[i], conv_norm_g[i], conv_norm_b[i], w_pw2[i], b_pw2[i])
            vertical = (i % 2 == 1)
            yp = odd_mixer(hp, *params, dwconv)
            ys = odd_mixer(hs, *params, lambda z, w, b: grid_dwconv(z, w, b, vertical))
        xp = xp + mp[2] * rms_norm(yp, g[1])
        xs = xs + ms[2] * rms_norm(ys, g[1])
        hp = rms_norm(xp, g[2]) * (1.0 + mp[4]) + mp[3]
        hs = rms_norm(xs, g[2]) * (1.0 + ms[4]) + ms[3]
        xp = xp + mp[5] * rms_norm(swiglu(hp, w_ffn_in[layer], w_ffn_out[layer]), g[3])
        xs = xs + ms[5] * rms_norm(swiglu(hs, w_ffn_in[layer], w_ffn_out[layer]), g[3])
    state_gla_new = jnp.stack(new_states, axis=1).astype(x_prompt.dtype)
    return (xp, xs, state_gla_new)
```

```python
import functools

import jax
import jax.numpy as jnp
from jax import lax
from jax.experimental import pallas as pl
from jax.experimental.pallas import tpu as pltpu

F32 = jnp.float32
BF16 = jnp.bfloat16

D = 2048
BATCH = 16
SEQ = 256
DEPTH = 4
DEC_BATCH = 4
DEC_SEQ = 2048
GRID_W = 64
HEADS = 4
DK = 128
DV = 256
RANK = 16
TAU = 16.0
CHUNK = 64
SG_GROUPS = 4
SG_CH = 256
SG_CHUNK = 128
QK_W = HEADS * DK
V_W = HEADS * DV
SG_W = SG_GROUPS * SG_CH
CONV_K = 31
HALF_K = CONV_K // 2
D_FF = 5632
N_MOD = 6
EPS = 1e-6

P_ROWS = BATCH * SEQ
S_ROWS = DEC_BATCH * DEC_SEQ
ROWS = P_ROWS + S_ROWS
MOD_ROWS = 8
Z1_W = 2 * QK_W + 2 * V_W + 128

BM_IN = 1024
BM_OUT = 512
VMEM_LIMIT = 56 * 1024 * 1024


def _params(sem):
    return pltpu.CompilerParams(dimension_semantics=sem, vmem_limit_bytes=VMEM_LIMIT)


def _dot(a, b):
    return jnp.dot(a, b, preferred_element_type=F32)


def _dot_nt(a, b):
    return lax.dot_general(a, b, (((1,), (1,)), ((), ())), preferred_element_type=F32)


def _dot_tn(a, b):
    return lax.dot_general(a, b, (((0,), (0,)), ((), ())), preferred_element_type=F32)


def _silu(x):
    return x / (1.0 + jnp.exp(-x))


def _rms(x, g):
    return x * lax.rsqrt(jnp.mean(x * x, axis=-1, keepdims=True) + EPS) * g


def _layer_norm(x, g, b):
    mu = jnp.mean(x, axis=-1, keepdims=True)
    xc = x - mu
    var = jnp.mean(xc * xc, axis=-1, keepdims=True)
    return xc * lax.rsqrt(var + EPS) * g + b


def _mod_row(i, bm):
    row0 = i * bm
    return jnp.where(row0 < P_ROWS, 0, 1 + (row0 - P_ROWS) // DEC_SEQ)


def _mod_spec(layer, chunk, bm):
    def index(i, *_):
        return ((layer * MOD_ROWS + _mod_row(i, bm)) * N_MOD + chunk, 0, 0)
    return pl.BlockSpec((1, 1, D), index)


def _row_vec_spec():
    return pl.BlockSpec((1, D), lambda i, *_: (0, 0))


def _mod_kernel(cv_ref, w_ref, b_ref, o_ref):
    s = _silu(cv_ref[...]).astype(BF16)
    o_ref[0] = _dot(s, w_ref[0].astype(BF16)) + b_ref[0]


def _modulation(cv, w_mod, b_mod):
    bn = 1024
    n = N_MOD * D
    return pl.pallas_call(
        _mod_kernel,
        grid=(DEPTH, n // bn),
        in_specs=[
            pl.BlockSpec((MOD_ROWS, D), lambda l, j: (0, 0)),
            pl.BlockSpec((1, D, bn), lambda l, j: (l, 0, j)),
            pl.BlockSpec((1, 1, bn), lambda l, j: (l, 0, j)),
        ],
        out_specs=pl.BlockSpec((1, MOD_ROWS, bn), lambda l, j: (l, 0, j)),
        out_shape=jax.ShapeDtypeStruct((DEPTH, MOD_ROWS, n), F32),
        compiler_params=_params(("arbitrary", "arbitrary")),
    )(cv, w_mod, b_mod.reshape(DEPTH, 1, n))


def _norm_kernel(x_ref, g_ref, sh_ref, sc_ref, h_ref):
    h = _rms(x_ref[...], g_ref[...]) * (1.0 + sc_ref[0]) + sh_ref[0]
    h_ref[...] = h.astype(BF16)


def _first_norm(x, g, mod):
    bm = BM_OUT
    return pl.pallas_call(
        _norm_kernel,
        grid=(ROWS // bm,),
        in_specs=[
            pl.BlockSpec((bm, D), lambda i: (i, 0)),
            _row_vec_spec(),
            _mod_spec(0, 0, bm),
            _mod_spec(0, 1, bm),
        ],
        out_specs=pl.BlockSpec((bm, D), lambda i: (i, 0)),
        out_shape=jax.ShapeDtypeStruct((ROWS, D), BF16),
        compiler_params=_params(("arbitrary",)),
    )(x, g, mod, mod)


def _mm_kernel(h_ref, w_ref, o_ref):
    o_ref[...] = _dot(h_ref[...], w_ref[...]).astype(o_ref.dtype)


def _matmul(h, w, bn, out_dtype):
    bm = BM_IN
    n = w.shape[1]
    return pl.pallas_call(
        _mm_kernel,
        grid=(ROWS // bm, n // bn),
        in_specs=[
            pl.BlockSpec((bm, D), lambda i, j: (i, 0)),
            pl.BlockSpec((D, bn), lambda i, j: (0, j)),
        ],
        out_specs=pl.BlockSpec((bm, bn), lambda i, j: (i, j)),
        out_shape=jax.ShapeDtypeStruct((ROWS, n), out_dtype),
        compiler_params=_params(("arbitrary", "arbitrary")),
    )(h, w)


def _swiglu_kernel(h_ref, wa_ref, wb_ref, o_ref):
    h = h_ref[...]
    a = _dot(h, wa_ref[...])
    b = _dot(h, wb_ref[...])
    o_ref[...] = (_silu(a) * b).astype(o_ref.dtype)


def _ffn_in(h, w):
    bm, bn = BM_IN, 512
    nb = D_FF // bn
    return pl.pallas_call(
        _swiglu_kernel,
        grid=(ROWS // bm, nb),
        in_specs=[
            pl.BlockSpec((bm, D), lambda i, j: (i, 0)),
            pl.BlockSpec((D, bn), lambda i, j: (0, j)),
            pl.BlockSpec((D, bn), lambda i, j: (0, j + nb)),
        ],
        out_specs=pl.BlockSpec((bm, bn), lambda i, j: (i, j)),
        out_shape=jax.ShapeDtypeStruct((ROWS, D_FF), BF16),
        compiler_params=_params(("arbitrary", "arbitrary")),
    )(h, w, w)


def _glu_kernel(h_ref, wa_ref, wb_ref, ba_ref, bb_ref, o_ref):
    h = h_ref[...]
    a = _dot(h, wa_ref[...]) + ba_ref[...]
    gt = _dot(h, wb_ref[...]) + bb_ref[...]
    o_ref[...] = a * (1.0 / (1.0 + jnp.exp(-gt)))


def _pw1_glu(h, w, b):
    bm, bn = BM_IN, 512
    nb = D // bn
    b2 = b.reshape(1, 2 * D)
    return pl.pallas_call(
        _glu_kernel,
        grid=(ROWS // bm, nb),
        in_specs=[
            pl.BlockSpec((bm, D), lambda i, j: (i, 0)),
            pl.BlockSpec((D, bn), lambda i, j: (0, j)),
            pl.BlockSpec((D, bn), lambda i, j: (0, j + nb)),
            pl.BlockSpec((1, bn), lambda i, j: (0, j)),
            pl.BlockSpec((1, bn), lambda i, j: (0, j + nb)),
        ],
        out_specs=pl.BlockSpec((bm, bn), lambda i, j: (i, j)),
        out_shape=jax.ShapeDtypeStruct((ROWS, D), F32),
        compiler_params=_params(("arbitrary", "arbitrary")),
    )(h, w, w, b2, b2)


def _gelu_tanh(x):
    return 0.5 * x * (1.0 + jnp.tanh(0.7978845608028654 * (x + 0.044715 * (x * x * x))))


def _sg_kernel(h_ref, wu_ref, wv_ref, g_ref, b_ref, wsp_ref, bsp_ref, o_ref):
    h = h_ref[...]
    u = _gelu_tanh(_dot(h, wu_ref[...]))
    vv = _gelu_tanh(_dot(h, wv_ref[...]))
    vv = _layer_norm(vv, g_ref[0], b_ref[0]).astype(BF16)
    wsp = wsp_ref[0]
    bsp = bsp_ref[0]
    for c in range(h.shape[0] // SG_CHUNK):
        rows = slice(c * SG_CHUNK, (c + 1) * SG_CHUNK)
        mixed = _dot(wsp, vv[rows]) + bsp
        o_ref[rows, :] = (u[rows] * mixed).astype(o_ref.dtype)


def _spatial_gating(h, w_sg, sg_g, sg_b, w_sp, b_sp):
    bm = BM_IN
    return pl.pallas_call(
        _sg_kernel,
        grid=(ROWS // bm, SG_GROUPS),
        in_specs=[
            pl.BlockSpec((bm, D), lambda i, g: (i, 0)),
            pl.BlockSpec((D, SG_CH), lambda i, g: (0, g)),
            pl.BlockSpec((D, SG_CH), lambda i, g: (0, g + SG_GROUPS)),
            pl.BlockSpec((1, 1, SG_CH), lambda i, g: (g, 0, 0)),
            pl.BlockSpec((1, 1, SG_CH), lambda i, g: (g, 0, 0)),
            pl.BlockSpec((1, SG_CHUNK, SG_CHUNK), lambda i, g: (g, 0, 0)),
            pl.BlockSpec((1, SG_CHUNK, 1), lambda i, g: (g, 0, 0)),
        ],
        out_specs=pl.BlockSpec((bm, SG_CH), lambda i, g: (i, g)),
        out_shape=jax.ShapeDtypeStruct((ROWS, SG_W), BF16),
        compiler_params=_params(("arbitrary", "arbitrary")),
    )(h, w_sg, w_sg, sg_g.reshape(SG_GROUPS, 1, SG_CH), sg_b.reshape(SG_GROUPS, 1, SG_CH),
      w_sp.astype(BF16), b_sp.reshape(SG_GROUPS, SG_CHUNK, 1))


def _split3(x):
    hi = x.astype(BF16)
    r1 = x - hi.astype(F32)
    mid = r1.astype(BF16)
    lo = (r1 - mid.astype(F32)).astype(BF16)
    return hi, mid, lo


def _tri_sum(tri, x):
    hi, mid, lo = _split3(x)
    return _dot(tri, hi) + _dot(tri, mid) + _dot(tri, lo)


def _gla_kernel(*refs, seq, has_init, emit_state):
    q_ref, k_ref, v_ref, r_ref, g_ref, wg_ref, bg_ref, gg_ref = refs[:8]
    pos = 8
    s0_ref = None
    if has_init:
        s0_ref = refs[pos]
        pos += 1
    o_ref = refs[pos]
    pos += 1
    sfin_ref = None
    if emit_state:
        sfin_ref = refs[pos]
        pos += 1
    la_scr, of_scr, ob_scr = refs[pos:pos + 3]

    n_chunks = seq // CHUNK
    scale = DK ** -0.5

    logits = _dot(g_ref[...].astype(BF16), wg_ref[0]) + bg_ref[0]
    la_scr[...] = (jnp.minimum(logits, 0.0) - jnp.log1p(jnp.exp(-jnp.abs(logits)))) / TAU

    ii = lax.broadcasted_iota(jnp.int32, (CHUNK, CHUNK), 0)
    jj = lax.broadcasted_iota(jnp.int32, (CHUNK, CHUNK), 1)
    lower = jj <= ii
    upper = jj >= ii
    tri_f = lower.astype(BF16)
    tri_b = upper.astype(BF16)

    def chunk_step(c0, la, mask, tri, last_row, st, out_scr):
        rows = pl.ds(c0, CHUNK)
        b = _tri_sum(tri, la)
        b_last = b[last_row:last_row + 1, :]
        q = q_ref[rows, :] * scale
        k = k_ref[rows, :]
        v = v_ref[rows, :].astype(BF16)
        q_in = (q * jnp.exp(b)).astype(BF16)
        k_in = (k * jnp.exp(-b)).astype(BF16)
        k_out = (k * jnp.exp(b_last - b)).astype(BF16)
        att = jnp.where(mask, _dot_nt(q_in, k_in), 0.0).astype(BF16)
        out_scr[rows, :] = _dot(att, v) + _dot_nt(q_in, st.astype(BF16))
        return st * jnp.exp(b_last) + _dot_tn(v, k_out)

    def body(c, carry):
        st_f, st_b = carry
        cf = pl.multiple_of(c * CHUNK, CHUNK)
        cb = pl.multiple_of((n_chunks - 1 - c) * CHUNK, CHUNK)
        st_f = chunk_step(cf, la_scr[pl.ds(cf, CHUNK), 0:DK], lower, tri_f, CHUNK - 1, st_f, of_scr)
        st_b = chunk_step(cb, la_scr[pl.ds(cb, CHUNK), DK:2 * DK], upper, tri_b, 0, st_b, ob_scr)
        return st_f, st_b

    if has_init:
        init = (s0_ref[0].T, s0_ref[1].T)
    else:
        init = (jnp.zeros((DV, DK), F32), jnp.zeros((DV, DK), F32))
    st_f, st_b = lax.fori_loop(0, n_chunks, body, init)

    if emit_state:
        sfin_ref[0] = st_f.T
        sfin_ref[1] = st_b.T

    blk = 256
    gg = gg_ref[0]

    def finish(i, carry):
        rows = pl.ds(pl.multiple_of(i * blk, blk), blk)
        o = _rms(of_scr[rows, :] + ob_scr[rows, :], gg)
        o_ref[rows, :] = (o * _silu(r_ref[rows, :])).astype(o_ref.dtype)
        return carry

    lax.fori_loop(0, seq // blk, finish, 0)


def _gla(z1, wg, bg, gla_g, seq, row_block0, n_seq, s0=None, layer_idx=0, emit_state=False):
    qk_blocks = QK_W // DK
    v_block0 = 2 * QK_W // DV
    r_block0 = v_block0 + HEADS
    g_block = (2 * QK_W + 2 * V_W) // DK
    rb = row_block0
    in_specs = [
        pl.BlockSpec((seq, DK), lambda b, h: (rb + b, h)),
        pl.BlockSpec((seq, DK), lambda b, h: (rb + b, qk_blocks + h)),
        pl.BlockSpec((seq, DV), lambda b, h: (rb + b, v_block0 + h)),
        pl.BlockSpec((seq, DV), lambda b, h: (rb + b, r_block0 + h)),
        pl.BlockSpec((seq, DK), lambda b, h: (rb + b, g_block)),
        pl.BlockSpec((1, DK, 2 * DK), lambda b, h: (h, 0, 0)),
        pl.BlockSpec((1, 1, 2 * DK), lambda b, h: (h, 0, 0)),
        pl.BlockSpec((1, 1, DV), lambda b, h: (h, 0, 0)),
    ]
    args = [z1, z1, z1, z1, z1, wg, bg, gla_g.reshape(HEADS, 1, DV)]
    if s0 is not None:
        in_specs.append(pl.BlockSpec((None, None, 2, None, DK, DV),
                                     lambda b, h: (b, layer_idx, 0, h, 0, 0)))
        args.append(s0)
    out_specs = [pl.BlockSpec((seq, DV), lambda b, h: (b, h))]
    out_shape = [jax.ShapeDtypeStruct((n_seq * seq, V_W), BF16)]
    if emit_state:
        out_specs.append(pl.BlockSpec((None, 2, None, DK, DV), lambda b, h: (b, 0, h, 0, 0)))
        out_shape.append(jax.ShapeDtypeStruct((n_seq, 2, HEADS, DK, DV), F32))
    kern = functools.partial(_gla_kernel, seq=seq, has_init=s0 is not None, emit_state=emit_state)
    return pl.pallas_call(
        kern,
        grid=(n_seq, HEADS),
        in_specs=in_specs,
        out_specs=out_specs,
        out_shape=out_shape,
        scratch_shapes=[
            pltpu.VMEM((seq, 2 * DK), F32),
            pltpu.VMEM((seq, DV), F32),
            pltpu.VMEM((seq, DV), F32),
        ],
        compiler_params=_params(("arbitrary", "arbitrary")),
    )(*args)


def _conv_rows_kernel(z_ref, w_ref, b_ref, o_ref, pad_scr, *, seg):
    rows, ct = z_ref.shape
    halo = 16
    zeros = jnp.zeros((halo, ct), F32)
    for s in range(rows // seg):
        pad_scr[0:halo, :] = zeros
        pad_scr[halo:halo + seg, :] = z_ref[s * seg:(s + 1) * seg, :]
        pad_scr[halo + seg:2 * halo + seg, :] = zeros
        acc = jnp.zeros((seg, ct), F32) + b_ref[...]
        for t in range(CONV_K):
            start = halo + t - HALF_K
            acc = acc + pad_scr[start:start + seg, :] * w_ref[t:t + 1, :]
        o_ref[s * seg:(s + 1) * seg, :] = acc


def _conv_rows(z, w, b, row0, n_rows, seg, ct):
    rows = 256
    rb0 = row0 // rows
    return pl.pallas_call(
        functools.partial(_conv_rows_kernel, seg=seg),
        grid=(n_rows // rows, D // ct),
        in_specs=[
            pl.BlockSpec((rows, ct), lambda i, j: (rb0 + i, j)),
            pl.BlockSpec((CONV_K, ct), lambda i, j: (0, j)),
            pl.BlockSpec((1, ct), lambda i, j: (0, j)),
        ],
        out_specs=pl.BlockSpec((rows, ct), lambda i, j: (i, j)),
        out_shape=jax.ShapeDtypeStruct((n_rows, D), F32),
        scratch_shapes=[pltpu.VMEM((seg + 32, ct), F32)],
        compiler_params=_params(("arbitrary", "arbitrary")),
    )(z, w, b.reshape(1, D))


def _conv_cols_kernel(z_ref, w_ref, b_ref, o_ref, pad_scr):
    rows, ct = z_ref.shape
    halo = HALF_K * GRID_W
    zeros = jnp.zeros((halo, ct), F32)
    pad_scr[0:halo, :] = zeros
    pad_scr[halo:halo + rows, :] = z_ref[...]
    pad_scr[halo + rows:2 * halo + rows, :] = zeros
    blk = 256

    def body(i, carry):
        r0 = pl.multiple_of(i * blk, blk)
        acc = jnp.zeros((blk, ct), F32) + b_ref[...]
        for t in range(CONV_K):
            acc = acc + pad_scr[pl.ds(r0 + t * GRID_W, blk), :] * w_ref[t:t + 1, :]
        o_ref[pl.ds(r0, blk), :] = acc
        return carry

    lax.fori_loop(0, rows // blk, body, 0)


def _conv_cols(z, w, b, row0, n_rows, ct):
    rows = DEC_SEQ
    rb0 = row0 // rows
    return pl.pallas_call(
        _conv_cols_kernel,
        grid=(n_rows // rows, D // ct),
        in_specs=[
            pl.BlockSpec((rows, ct), lambda i, j: (rb0 + i, j)),
            pl.BlockSpec((CONV_K, ct), lambda i, j: (0, j)),
            pl.BlockSpec((1, ct), lambda i, j: (0, j)),
        ],
        out_specs=pl.BlockSpec((rows, ct), lambda i, j: (i, j)),
        out_shape=jax.ShapeDtypeStruct((n_rows, D), F32),
        scratch_shapes=[pltpu.VMEM((rows + 2 * HALF_K * GRID_W, ct), F32)],
        compiler_params=_params(("arbitrary", "arbitrary")),
    )(z, w, b.reshape(1, D))


def _out_kernel(*refs, n_lhs, nk, split_rows, ln_prologue, has_bias, emit_h):
    pos = 0
    lhs_refs = refs[pos:pos + n_lhs]
    pos += n_lhs
    w_ref = refs[pos]
    pos += 1
    if ln_prologue:
        lg_ref, lb_ref = refs[pos:pos + 2]
        pos += 2
    if has_bias:
        bias_ref = refs[pos]
        pos += 1
    x_ref, ga_ref, gate_ref = refs[pos:pos + 3]
    pos += 3
    if emit_h:
        gb_ref, sh_ref, sc_ref = refs[pos:pos + 3]
        pos += 3
    xo_ref = refs[pos]
    pos += 1
    if emit_h:
        h_ref = refs[pos]
        pos += 1
    acc_ref = refs[pos]

    i = pl.program_id(0)
    k = pl.program_id(1)

    def lhs_value(ref):
        v = ref[...]
        if ln_prologue:
            v = _silu(_layer_norm(v, lg_ref[...], lb_ref[...]))
        return v.astype(BF16)

    def accumulate(ref):
        part = _dot(lhs_value(ref), w_ref[...])

        @pl.when(k == 0)
        def _():
            acc_ref[...] = part

        @pl.when(k > 0)
        def _():
            acc_ref[...] += part

    if split_rows is None:
        accumulate(lhs_refs[0])
    else:
        n_first, p_tiles = split_rows
        pl.when((k < n_first) & (i < p_tiles))(lambda: accumulate(lhs_refs[0]))
        pl.when((k < n_first) & (i >= p_tiles))(lambda: accumulate(lhs_refs[1]))
        if n_lhs == 3:
            pl.when(k >= n_first)(lambda: accumulate(lhs_refs[2]))

    @pl.when(k == nk - 1)
    def _():
        y = acc_ref[...]
        if has_bias:
            y = y + bias_ref[...]
        x_new = x_ref[...] + gate_ref[0] * _rms(y, ga_ref[...])
        xo_ref[...] = x_new
        if emit_h:
            h = _rms(x_new, gb_ref[...]) * (1.0 + sc_ref[0]) + sh_ref[0]
            h_ref[...] = h.astype(BF16)


def _out_proj(lhs, lhs_specs, w, bk, x, mod, g_y, gate_idx, next_norm, *, split_rows=None,
              ln=None, bias=None, bm=BM_OUT):
    kk = w.shape[0]
    nk = kk // bk
    emit_h = next_norm is not None
    in_specs = list(lhs_specs) + [pl.BlockSpec((bk, D), lambda i, k: (k, 0))]
    args = list(lhs) + [w]
    if ln is not None:
        in_specs += [_row_vec_spec(), _row_vec_spec()]
        args += [ln[0].reshape(1, D), ln[1].reshape(1, D)]
    if bias is not None:
        in_specs.append(_row_vec_spec())
        args.append(bias.reshape(1, D))
    in_specs += [pl.BlockSpec((bm, D), lambda i, k: (i, 0)), _row_vec_spec(),
                 _mod_spec(gate_idx[0], gate_idx[1], bm)]
    args += [x, g_y.reshape(1, D), mod]
    out_specs = [pl.BlockSpec((bm, D), lambda i, k: (i, 0))]
    out_shape = [jax.ShapeDtypeStruct((ROWS, D), F32)]
    if emit_h:
        gain, (layer, shift_chunk, scale_chunk) = next_norm
        in_specs += [_row_vec_spec(), _mod_spec(layer, shift_chunk, bm), _mod_spec(layer, scale_chunk, bm)]
        args += [gain.reshape(1, D), mod, mod]
        out_specs.append(pl.BlockSpec((bm, D), lambda i, k: (i, 0)))
        out_shape.append(jax.ShapeDtypeStruct((ROWS, D), BF16))
    kern = functools.partial(_out_kernel, n_lhs=len(lhs), nk=nk, split_rows=split_rows,
                             ln_prologue=ln is not None, has_bias=bias is not None, emit_h=emit_h)
    res = pl.pallas_call(
        kern,
        grid=(ROWS // bm, nk),
        in_specs=in_specs,
        out_specs=out_specs,
        out_shape=out_shape,
        scratch_shapes=[pltpu.VMEM((bm, D), F32)],
        compiler_params=_params(("arbitrary", "arbitrary")),
    )(*args)
    return (res[0], res[1]) if emit_h else (res[0], None)


def _split_row_specs(bm, width_blocks_first, bk):
    p_tiles = P_ROWS // bm
    s_tiles = S_ROWS // bm
    last = width_blocks_first - 1
    spec_p = pl.BlockSpec((bm, bk), lambda i, k: (jnp.minimum(i, p_tiles - 1), jnp.minimum(k, last)))
    spec_s = pl.BlockSpec(
        (bm, bk), lambda i, k: (jnp.clip(i - p_tiles, 0, s_tiles - 1), jnp.minimum(k, last)))
    return spec_p, spec_s, p_tiles


def kernel(x_prompt, x_sample, state_gla, c, c_ctx, norm_g, w_mod, b_mod, w_in_a, w_gate_up, b_gate, gla_norm_g, sg_norm_g, sg_norm_b, w_spatial, b_spatial, w_out_a, w_pw1, b_pw1, w_dw, b_dw, conv_norm_g, conv_norm_b, w_pw2, b_pw2, w_ffn_in, w_ffn_out):
    x = jnp.concatenate([x_prompt.reshape(P_ROWS, D), x_sample.reshape(S_ROWS, D)], axis=0)
    cv = jnp.concatenate([c_ctx[None, :], c, jnp.zeros((MOD_ROWS - 1 - DEC_BATCH, D), F32)], axis=0)
    mod = _modulation(cv, w_mod, b_mod).reshape(DEPTH * MOD_ROWS * N_MOD, 1, D)

    h = _first_norm(x, norm_g[0, 0].reshape(1, D), mod)
    new_states = []
    bm = BM_OUT
    for layer in range(DEPTH):
        g = norm_g[layer]
        i = layer // 2
        mid_norm = (g[2], (layer, 3, 4))
        if layer % 2 == 0:
            w_in = w_in_a[i]
            n_z1 = 2 * QK_W + 2 * V_W + 2 * RANK
            w_z1 = jnp.pad(w_in[:, :n_z1], ((0, 0), (0, Z1_W - n_z1))).astype(BF16)
            w_sg = w_in[:, n_z1:].astype(BF16)
            z1 = _matmul(h, w_z1, 640, F32)
            sg_out = _spatial_gating(h, w_sg, sg_norm_g[i], sg_norm_b[i], w_spatial[i], b_spatial[i])

            wf = w_gate_up[i, 0].reshape(RANK, HEADS, DK).transpose(1, 0, 2)
            wb = w_gate_up[i, 1].reshape(RANK, HEADS, DK).transpose(1, 0, 2)
            wg = jnp.zeros((HEADS, DK, 2 * DK), F32)
            wg = wg.at[:, 0:RANK, 0:DK].set(wf).at[:, RANK:2 * RANK, DK:2 * DK].set(wb).astype(BF16)
            bg = jnp.concatenate([b_gate[i, 0].reshape(HEADS, 1, DK), b_gate[i, 1].reshape(HEADS, 1, DK)],
                                 axis=-1)
            o_p, s_new = _gla(z1, wg, bg, gla_norm_g[i], SEQ, 0, BATCH, emit_state=True)
            (o_s,) = _gla(z1, wg, bg, gla_norm_g[i], DEC_SEQ, P_ROWS // DEC_SEQ, DEC_BATCH,
                          s0=state_gla, layer_idx=i)
            new_states.append(s_new)

            bk = 1024
            spec_p, spec_s, p_tiles = _split_row_specs(bm, V_W // bk, bk)
            n_first = V_W // bk
            spec_sg = pl.BlockSpec((bm, bk), lambda r, k: (r, jnp.maximum(k - n_first, 0)))
            x, h = _out_proj([o_p, o_s, sg_out], [spec_p, spec_s, spec_sg], w_out_a[i].astype(BF16), bk,
                             x, mod, g[1], (layer, 2), mid_norm, split_rows=(n_first, p_tiles))
        else:
            z = _pw1_glu(h, w_pw1[i].astype(BF16), b_pw1[i])
            conv_p = _conv_rows(z, w_dw[i], b_dw[i], 0, P_ROWS, SEQ, 128)
            if i % 2 == 1:
                conv_s = _conv_cols(z, w_dw[i], b_dw[i], P_ROWS, S_ROWS, 128)
            else:
                conv_s = _conv_rows(z, w_dw[i], b_dw[i], P_ROWS, S_ROWS, GRID_W, 512)
            bm_pw2 = 256
            spec_p, spec_s, p_tiles = _split_row_specs(bm_pw2, 1, D)
            x, h = _out_proj([conv_p, conv_s], [spec_p, spec_s], w_pw2[i].astype(BF16), D,
                             x, mod, g[1], (layer, 2), mid_norm, split_rows=(1, p_tiles),
                             ln=(conv_norm_g[i], conv_norm_b[i]), bias=b_pw2[i], bm=bm_pw2)

        act = _ffn_in(h, w_ffn_in[layer].astype(BF16))
        bk = 1408
        next_norm = None if layer == DEPTH - 1 else (norm_g[layer + 1, 0], (layer + 1, 0, 1))
        x, h = _out_proj([act], [pl.BlockSpec((bm, bk), lambda r, k: (r, k))],
                         w_ffn_out[layer].astype(BF16), bk, x, mod, g[3], (layer, 5), next_norm)

    y_prompt = x[:P_ROWS].reshape(BATCH, SEQ, D)
    y_sample = x[P_ROWS:].reshape(DEC_BATCH, DEC_SEQ, D)
    state_new = jnp.stack(new_states, axis=1).astype(x_prompt.dtype)
    return (y_prompt, y_sample, state_new)
```

```python
import functools

import jax
import jax.numpy as jnp
from jax import lax
from jax.experimental import pallas as pl
from jax.experimental.pallas import tpu as pltpu

F32 = jnp.float32
BF16 = jnp.bfloat16

D = 2048
BATCH = 16
SEQ = 256
DEPTH = 4
DEC_BATCH = 4
DEC_SEQ = 2048
GRID_W = 64
HEADS = 4
DK = 128
DV = 256
RANK = 16
TAU = 16.0
CHUNK = 64
SG_GROUPS = 4
SG_CH = 256
SG_CHUNK = 128
QK_W = HEADS * DK
V_W = HEADS * DV
SG_W = SG_GROUPS * SG_CH
CONV_K = 31
HALF_K = CONV_K // 2
D_FF = 5632
N_MOD = 6
EPS = 1e-6

P_ROWS = BATCH * SEQ
S_ROWS = DEC_BATCH * DEC_SEQ
ROWS = P_ROWS + S_ROWS
MOD_ROWS = 8
Z1_W = 2 * QK_W + 2 * V_W + 128

BM_IN = 1024
BM_OUT = 1024
EPI_ROWS = 16
VMEM_LIMIT = 56 * 1024 * 1024


def _params(sem):
    return pltpu.CompilerParams(dimension_semantics=sem, vmem_limit_bytes=VMEM_LIMIT)


def _dot(a, b):
    return jnp.dot(a, b, preferred_element_type=F32)


def _dot_nt(a, b):
    return lax.dot_general(a, b, (((1,), (1,)), ((), ())), preferred_element_type=F32)


def _dot_tn(a, b):
    return lax.dot_general(a, b, (((0,), (0,)), ((), ())), preferred_element_type=F32)


def _silu(x):
    return x / (1.0 + jnp.exp(-x))


def _rms(x, g):
    return x * lax.rsqrt(jnp.mean(x * x, axis=-1, keepdims=True) + EPS) * g


def _layer_norm(x, g, b):
    mu = jnp.mean(x, axis=-1, keepdims=True)
    xc = x - mu
    var = jnp.mean(xc * xc, axis=-1, keepdims=True)
    return xc * lax.rsqrt(var + EPS) * g + b


def _mod_row(i, bm):
    row0 = i * bm
    return jnp.where(row0 < P_ROWS, 0, 1 + (row0 - P_ROWS) // DEC_SEQ)


def _mod_spec(layer, chunk, bm):
    def index(i, *_):
        return ((layer * MOD_ROWS + _mod_row(i, bm)) * N_MOD + chunk, 0, 0)
    return pl.BlockSpec((1, 1, D), index)


def _row_vec_spec():
    return pl.BlockSpec((1, D), lambda i, *_: (0, 0))


def _mod_kernel(cv_ref, w_ref, b_ref, o_ref):
    s = _silu(cv_ref[...]).astype(BF16)
    o_ref[0] = _dot(s, w_ref[0].astype(BF16)) + b_ref[0]


def _modulation(cv, w_mod, b_mod):
    bn = 1024
    n = N_MOD * D
    return pl.pallas_call(
        _mod_kernel,
        grid=(DEPTH, n // bn),
        in_specs=[
            pl.BlockSpec((MOD_ROWS, D), lambda l, j: (0, 0)),
            pl.BlockSpec((1, D, bn), lambda l, j: (l, 0, j)),
            pl.BlockSpec((1, 1, bn), lambda l, j: (l, 0, j)),
        ],
        out_specs=pl.BlockSpec((1, MOD_ROWS, bn), lambda l, j: (l, 0, j)),
        out_shape=jax.ShapeDtypeStruct((DEPTH, MOD_ROWS, n), F32),
        name="modulation",
        compiler_params=_params(("arbitrary", "arbitrary")),
    )(cv, w_mod, b_mod.reshape(DEPTH, 1, n))


def _norm_kernel(x_ref, g_ref, sh_ref, sc_ref, h_ref):
    h = _rms(x_ref[...], g_ref[...]) * (1.0 + sc_ref[0]) + sh_ref[0]
    h_ref[...] = h.astype(BF16)


def _first_norm(x, g, mod):
    bm = 256
    return pl.pallas_call(
        _norm_kernel,
        grid=(ROWS // bm,),
        in_specs=[
            pl.BlockSpec((bm, D), lambda i: (i, 0)),
            _row_vec_spec(),
            _mod_spec(0, 0, bm),
            _mod_spec(0, 1, bm),
        ],
        out_specs=pl.BlockSpec((bm, D), lambda i: (i, 0)),
        out_shape=jax.ShapeDtypeStruct((ROWS, D), BF16),
        name="first_norm",
        compiler_params=_params(("arbitrary",)),
    )(x, g, mod, mod)


def _mm_kernel(h_ref, w_ref, o_ref):
    o_ref[...] = _dot(h_ref[...], w_ref[...]).astype(o_ref.dtype)


def _matmul(h, w, bn, out_dtype):
    bm = BM_IN
    n = w.shape[1]
    return pl.pallas_call(
        _mm_kernel,
        grid=(ROWS // bm, n // bn),
        in_specs=[
            pl.BlockSpec((bm, D), lambda i, j: (i, 0)),
            pl.BlockSpec((D, bn), lambda i, j: (0, j)),
        ],
        out_specs=pl.BlockSpec((bm, bn), lambda i, j: (i, j)),
        out_shape=jax.ShapeDtypeStruct((ROWS, n), out_dtype),
        name="in_proj",
        compiler_params=_params(("arbitrary", "arbitrary")),
    )(h, w)


def _swiglu_kernel(h_ref, wa_ref, wb_ref, o_ref):
    h = h_ref[...]
    a = _dot(h, wa_ref[...])
    b = _dot(h, wb_ref[...])
    o_ref[...] = (_silu(a) * b).astype(o_ref.dtype)


def _ffn_in(h, w):
    bm, bn = BM_IN, 512
    nb = D_FF // bn
    return pl.pallas_call(
        _swiglu_kernel,
        grid=(ROWS // bm, nb),
        in_specs=[
            pl.BlockSpec((bm, D), lambda i, j: (i, 0)),
            pl.BlockSpec((D, bn), lambda i, j: (0, j)),
            pl.BlockSpec((D, bn), lambda i, j: (0, j + nb)),
        ],
        out_specs=pl.BlockSpec((bm, bn), lambda i, j: (i, j)),
        out_shape=jax.ShapeDtypeStruct((ROWS, D_FF), BF16),
        name="ffn_in",
        compiler_params=_params(("arbitrary", "arbitrary")),
    )(h, w, w)


def _glu_kernel(h_ref, wa_ref, wb_ref, ba_ref, bb_ref, o_ref):
    h = h_ref[...]
    a = _dot(h, wa_ref[...]) + ba_ref[...]
    gt = _dot(h, wb_ref[...]) + bb_ref[...]
    o_ref[...] = a * (1.0 / (1.0 + jnp.exp(-gt)))


def _pw1_glu(h, w, b):
    bm, bn = BM_IN, 512
    nb = D // bn
    b2 = b.reshape(1, 2 * D)
    return pl.pallas_call(
        _glu_kernel,
        grid=(ROWS // bm, nb),
        in_specs=[
            pl.BlockSpec((bm, D), lambda i, j: (i, 0)),
            pl.BlockSpec((D, bn), lambda i, j: (0, j)),
            pl.BlockSpec((D, bn), lambda i, j: (0, j + nb)),
            pl.BlockSpec((1, bn), lambda i, j: (0, j)),
            pl.BlockSpec((1, bn), lambda i, j: (0, j + nb)),
        ],
        out_specs=pl.BlockSpec((bm, bn), lambda i, j: (i, j)),
        out_shape=jax.ShapeDtypeStruct((ROWS, D), F32),
        name="pw1_glu",
        compiler_params=_params(("arbitrary", "arbitrary")),
    )(h, w, w, b2, b2)


def _gelu_tanh(x):
    return 0.5 * x * (1.0 + jnp.tanh(0.7978845608028654 * (x + 0.044715 * (x * x * x))))


def _sg_kernel(h_ref, wu_ref, wv_ref, g_ref, b_ref, wsp_ref, bsp_ref, o_ref):
    h = h_ref[...]
    u = _gelu_tanh(_dot(h, wu_ref[...]))
    vv = _gelu_tanh(_dot(h, wv_ref[...]))
    vv = _layer_norm(vv, g_ref[0], b_ref[0]).astype(BF16)
    wsp = wsp_ref[0]
    bsp = bsp_ref[0]
    for c in range(h.shape[0] // SG_CHUNK):
        rows = slice(c * SG_CHUNK, (c + 1) * SG_CHUNK)
        mixed = _dot(wsp, vv[rows]) + bsp
        o_ref[rows, :] = (u[rows] * mixed).astype(o_ref.dtype)


def _spatial_gating(h, w_sg, sg_g, sg_b, w_sp, b_sp):
    bm = BM_IN
    return pl.pallas_call(
        _sg_kernel,
        grid=(ROWS // bm, SG_GROUPS),
        in_specs=[
            pl.BlockSpec((bm, D), lambda i, g: (i, 0)),
            pl.BlockSpec((D, SG_CH), lambda i, g: (0, g)),
            pl.BlockSpec((D, SG_CH), lambda i, g: (0, g + SG_GROUPS)),
            pl.BlockSpec((1, 1, SG_CH), lambda i, g: (g, 0, 0)),
            pl.BlockSpec((1, 1, SG_CH), lambda i, g: (g, 0, 0)),
            pl.BlockSpec((1, SG_CHUNK, SG_CHUNK), lambda i, g: (g, 0, 0)),
            pl.BlockSpec((1, SG_CHUNK, 1), lambda i, g: (g, 0, 0)),
        ],
        out_specs=pl.BlockSpec((bm, SG_CH), lambda i, g: (i, g)),
        out_shape=jax.ShapeDtypeStruct((ROWS, SG_W), BF16),
        name="spatial_gating",
        compiler_params=_params(("arbitrary", "arbitrary")),
    )(h, w_sg, w_sg, sg_g.reshape(SG_GROUPS, 1, SG_CH), sg_b.reshape(SG_GROUPS, 1, SG_CH),
      w_sp.astype(BF16), b_sp.reshape(SG_GROUPS, SG_CHUNK, 1))


def _split3(x):
    hi = x.astype(BF16)
    r1 = x - hi.astype(F32)
    mid = r1.astype(BF16)
    lo = (r1 - mid.astype(F32)).astype(BF16)
    return hi, mid, lo


def _tri_sum(tri, x):
    hi, mid, lo = _split3(x)
    return _dot(tri, hi) + _dot(tri, mid) + _dot(tri, lo)


def _gla_kernel(*refs, seq, has_init, emit_state):
    q_ref, k_ref, v_ref, r_ref, g_ref, wg_ref, bg_ref, gg_ref = refs[:8]
    pos = 8
    s0_ref = None
    if has_init:
        s0_ref = refs[pos]
        pos += 1
    o_ref = refs[pos]
    pos += 1
    sfin_ref = None
    if emit_state:
        sfin_ref = refs[pos]
        pos += 1
    la_scr, of_scr, ob_scr = refs[pos:pos + 3]

    n_chunks = seq // CHUNK
    scale = DK ** -0.5

    logits = _dot(g_ref[...].astype(BF16), wg_ref[0]) + bg_ref[0]
    la_scr[...] = (jnp.minimum(logits, 0.0) - jnp.log1p(jnp.exp(-jnp.abs(logits)))) / TAU

    ii = lax.broadcasted_iota(jnp.int32, (CHUNK, CHUNK), 0)
    jj = lax.broadcasted_iota(jnp.int32, (CHUNK, CHUNK), 1)
    lower = jj <= ii
    upper = jj >= ii
    tri_f = lower.astype(BF16)
    tri_b = upper.astype(BF16)

    def chunk_step(c0, la, mask, tri, last_row, st, out_scr):
        rows = pl.ds(c0, CHUNK)
        b = _tri_sum(tri, la)
        b_last = b[last_row:last_row + 1, :]
        q = q_ref[rows, :] * scale
        k = k_ref[rows, :]
        v = v_ref[rows, :].astype(BF16)
        q_in = (q * jnp.exp(b)).astype(BF16)
        k_in = (k * jnp.exp(-b)).astype(BF16)
        k_out = (k * jnp.exp(b_last - b)).astype(BF16)
        att = jnp.where(mask, _dot_nt(q_in, k_in), 0.0).astype(BF16)
        out_scr[rows, :] = _dot(att, v) + _dot_nt(q_in, st.astype(BF16))
        return st * jnp.exp(b_last) + _dot_tn(v, k_out)

    def body(c, carry):
        st_f, st_b = carry
        cf = pl.multiple_of(c * CHUNK, CHUNK)
        cb = pl.multiple_of((n_chunks - 1 - c) * CHUNK, CHUNK)
        st_f = chunk_step(cf, la_scr[pl.ds(cf, CHUNK), 0:DK], lower, tri_f, CHUNK - 1, st_f, of_scr)
        st_b = chunk_step(cb, la_scr[pl.ds(cb, CHUNK), DK:2 * DK], upper, tri_b, 0, st_b, ob_scr)
        return st_f, st_b

    if has_init:
        init = (s0_ref[0].T, s0_ref[1].T)
    else:
        init = (jnp.zeros((DV, DK), F32), jnp.zeros((DV, DK), F32))
    st_f, st_b = lax.fori_loop(0, n_chunks, body, init, unroll=2)

    if emit_state:
        sfin_ref[0] = st_f.T
        sfin_ref[1] = st_b.T

    blk = 256
    gg = gg_ref[0]

    def finish(i, carry):
        rows = pl.ds(pl.multiple_of(i * blk, blk), blk)
        o = _rms(of_scr[rows, :] + ob_scr[rows, :], gg)
        o_ref[rows, :] = (o * _silu(r_ref[rows, :])).astype(o_ref.dtype)
        return carry

    lax.fori_loop(0, seq // blk, finish, 0)


def _gla(z1, wg, bg, gla_g, seq, row_block0, n_seq, s0=None, layer_idx=0, emit_state=False):
    qk_blocks = QK_W // DK
    v_block0 = 2 * QK_W // DV
    r_block0 = v_block0 + HEADS
    g_block = (2 * QK_W + 2 * V_W) // DK
    rb = row_block0
    in_specs = [
        pl.BlockSpec((seq, DK), lambda b, h: (rb + b, h)),
        pl.BlockSpec((seq, DK), lambda b, h: (rb + b, qk_blocks + h)),
        pl.BlockSpec((seq, DV), lambda b, h: (rb + b, v_block0 + h)),
        pl.BlockSpec((seq, DV), lambda b, h: (rb + b, r_block0 + h)),
        pl.BlockSpec((seq, DK), lambda b, h: (rb + b, g_block)),
        pl.BlockSpec((1, DK, 2 * DK), lambda b, h: (h, 0, 0)),
        pl.BlockSpec((1, 1, 2 * DK), lambda b, h: (h, 0, 0)),
        pl.BlockSpec((1, 1, DV), lambda b, h: (h, 0, 0)),
    ]
    args = [z1, z1, z1, z1, z1, wg, bg, gla_g.reshape(HEADS, 1, DV)]
    if s0 is not None:
        in_specs.append(pl.BlockSpec((None, None, 2, None, DK, DV),
                                     lambda b, h: (b, layer_idx, 0, h, 0, 0)))
        args.append(s0)
    out_specs = [pl.BlockSpec((seq, DV), lambda b, h: (b, h))]
    out_shape = [jax.ShapeDtypeStruct((n_seq * seq, V_W), BF16)]
    if emit_state:
        out_specs.append(pl.BlockSpec((None, 2, None, DK, DV), lambda b, h: (b, 0, h, 0, 0)))
        out_shape.append(jax.ShapeDtypeStruct((n_seq, 2, HEADS, DK, DV), F32))
    kern = functools.partial(_gla_kernel, seq=seq, has_init=s0 is not None, emit_state=emit_state)
    return pl.pallas_call(
        kern,
        grid=(n_seq, HEADS),
        in_specs=in_specs,
        out_specs=out_specs,
        out_shape=out_shape,
        scratch_shapes=[
            pltpu.VMEM((seq, 2 * DK), F32),
            pltpu.VMEM((seq, DV), F32),
            pltpu.VMEM((seq, DV), F32),
        ],
        name="gla_seq%d" % seq,
        compiler_params=_params(("arbitrary", "arbitrary")),
    )(*args)


def _conv_rows_kernel(z_ref, w_ref, b_ref, o_ref, pad_scr, *, seg):
    rows, ct = z_ref.shape
    halo = 16
    zeros = jnp.zeros((halo, ct), F32)
    for s in range(rows // seg):
        pad_scr[0:halo, :] = zeros
        pad_scr[halo:halo + seg, :] = z_ref[s * seg:(s + 1) * seg, :]
        pad_scr[halo + seg:2 * halo + seg, :] = zeros
        acc = jnp.zeros((seg, ct), F32) + b_ref[...]
        for t in range(CONV_K):
            start = halo + t - HALF_K
            acc = acc + pad_scr[start:start + seg, :] * w_ref[t:t + 1, :]
        o_ref[s * seg:(s + 1) * seg, :] = acc


def _conv_rows(z, w, b, row0, n_rows, seg, ct):
    rows = 256
    rb0 = row0 // rows
    return pl.pallas_call(
        functools.partial(_conv_rows_kernel, seg=seg),
        grid=(n_rows // rows, D // ct),
        in_specs=[
            pl.BlockSpec((rows, ct), lambda i, j: (rb0 + i, j)),
            pl.BlockSpec((CONV_K, ct), lambda i, j: (0, j)),
            pl.BlockSpec((1, ct), lambda i, j: (0, j)),
        ],
        out_specs=pl.BlockSpec((rows, ct), lambda i, j: (i, j)),
        out_shape=jax.ShapeDtypeStruct((n_rows, D), F32),
        scratch_shapes=[pltpu.VMEM((seg + 32, ct), F32)],
        name="conv_rows_seg%d" % seg,
        compiler_params=_params(("arbitrary", "arbitrary")),
    )(z, w, b.reshape(1, D))


def _conv_cols_kernel(z_ref, w_ref, b_ref, o_ref, pad_scr):
    rows, ct = z_ref.shape
    halo = HALF_K * GRID_W
    zeros = jnp.zeros((halo, ct), F32)
    pad_scr[0:halo, :] = zeros
    pad_scr[halo:halo + rows, :] = z_ref[...]
    pad_scr[halo + rows:2 * halo + rows, :] = zeros
    blk = 256

    def body(i, carry):
        r0 = pl.multiple_of(i * blk, blk)
        acc = jnp.zeros((blk, ct), F32) + b_ref[...]
        for t in range(CONV_K):
            acc = acc + pad_scr[pl.ds(r0 + t * GRID_W, blk), :] * w_ref[t:t + 1, :]
        o_ref[pl.ds(r0, blk), :] = acc
        return carry

    lax.fori_loop(0, rows // blk, body, 0)


def _conv_cols(z, w, b, row0, n_rows, ct):
    rows = DEC_SEQ
    rb0 = row0 // rows
    return pl.pallas_call(
        _conv_cols_kernel,
        grid=(n_rows // rows, D // ct),
        in_specs=[
            pl.BlockSpec((rows, ct), lambda i, j: (rb0 + i, j)),
            pl.BlockSpec((CONV_K, ct), lambda i, j: (0, j)),
            pl.BlockSpec((1, ct), lambda i, j: (0, j)),
        ],
        out_specs=pl.BlockSpec((rows, ct), lambda i, j: (i, j)),
        out_shape=jax.ShapeDtypeStruct((n_rows, D), F32),
        scratch_shapes=[pltpu.VMEM((rows + 2 * HALF_K * GRID_W, ct), F32)],
        name="conv_cols",
        compiler_params=_params(("arbitrary", "arbitrary")),
    )(z, w, b.reshape(1, D))


def _out_kernel(*refs, n_lhs, nk, split_rows, ln_prologue, has_bias, emit_h):
    pos = 0
    lhs_refs = refs[pos:pos + n_lhs]
    pos += n_lhs
    w_ref = refs[pos]
    pos += 1
    if ln_prologue:
        lg_ref, lb_ref = refs[pos:pos + 2]
        pos += 2
    if has_bias:
        bias_ref = refs[pos]
        pos += 1
    x_ref, ga_ref, gate_ref = refs[pos:pos + 3]
    pos += 3
    if emit_h:
        gb_ref, sh_ref, sc_ref = refs[pos:pos + 3]
        pos += 3
    xo_ref = refs[pos]
    pos += 1
    if emit_h:
        h_ref = refs[pos]

    i = pl.program_id(0)
    k = pl.program_id(1)

    def lhs_value(ref):
        v = ref[...]
        if ln_prologue:
            v = _silu(_layer_norm(v, lg_ref[...], lb_ref[...]))
        return v.astype(BF16)

    def accumulate(ref):
        part = _dot(lhs_value(ref), w_ref[...])
        if nk == 1:
            xo_ref[...] = part
            return

        @pl.when(k == 0)
        def _():
            xo_ref[...] = part

        @pl.when(k > 0)
        def _():
            xo_ref[...] += part

    if split_rows is None:
        accumulate(lhs_refs[0])
    else:
        n_first, p_tiles = split_rows
        pl.when((k < n_first) & (i < p_tiles))(lambda: accumulate(lhs_refs[0]))
        pl.when((k < n_first) & (i >= p_tiles))(lambda: accumulate(lhs_refs[1]))
        if n_lhs == 3:
            pl.when(k >= n_first)(lambda: accumulate(lhs_refs[2]))

    @pl.when(k == nk - 1)
    def _():
        def rows_step(r, carry):
            rows = pl.ds(pl.multiple_of(r * EPI_ROWS, EPI_ROWS), EPI_ROWS)
            y = xo_ref[rows, :]
            if has_bias:
                y = y + bias_ref[...]
            x_new = x_ref[rows, :] + gate_ref[0] * _rms(y, ga_ref[...])
            xo_ref[rows, :] = x_new
            if emit_h:
                h = _rms(x_new, gb_ref[...]) * (1.0 + sc_ref[0]) + sh_ref[0]
                h_ref[rows, :] = h.astype(BF16)
            return carry

        lax.fori_loop(0, xo_ref.shape[0] // EPI_ROWS, rows_step, 0)


def _out_proj(lhs, lhs_specs, w, bk, x, mod, g_y, gate_idx, next_norm, *, split_rows=None,
              ln=None, bias=None, bm=BM_OUT):
    kk = w.shape[0]
    nk = kk // bk
    emit_h = next_norm is not None
    in_specs = list(lhs_specs) + [pl.BlockSpec((bk, D), lambda i, k: (k, 0))]
    args = list(lhs) + [w]
    if ln is not None:
        in_specs += [_row_vec_spec(), _row_vec_spec()]
        args += [ln[0].reshape(1, D), ln[1].reshape(1, D)]
    if bias is not None:
        in_specs.append(_row_vec_spec())
        args.append(bias.reshape(1, D))
    in_specs += [pl.BlockSpec((bm, D), lambda i, k: (i, 0), pipeline_mode=pl.Buffered(1)),
                 _row_vec_spec(), _mod_spec(gate_idx[0], gate_idx[1], bm)]
    args += [x, g_y.reshape(1, D), mod]
    out_specs = [pl.BlockSpec((bm, D), lambda i, k: (i, 0))]
    out_shape = [jax.ShapeDtypeStruct((ROWS, D), F32)]
    if emit_h:
        gain, (layer, shift_chunk, scale_chunk) = next_norm
        in_specs += [_row_vec_spec(), _mod_spec(layer, shift_chunk, bm), _mod_spec(layer, scale_chunk, bm)]
        args += [gain.reshape(1, D), mod, mod]
        out_specs.append(pl.BlockSpec((bm, D), lambda i, k: (i, 0)))
        out_shape.append(jax.ShapeDtypeStruct((ROWS, D), BF16))
    kern = functools.partial(_out_kernel, n_lhs=len(lhs), nk=nk, split_rows=split_rows,
                             ln_prologue=ln is not None, has_bias=bias is not None, emit_h=emit_h)
    res = pl.pallas_call(
        kern,
        grid=(ROWS // bm, nk),
        in_specs=in_specs,
        out_specs=out_specs,
        out_shape=out_shape,
        name="out_proj_k%d" % kk,
        compiler_params=_params(("arbitrary", "arbitrary")),
    )(*args)
    return (res[0], res[1]) if emit_h else (res[0], None)


def _split_row_specs(bm, width_blocks_first, bk):
    p_tiles = P_ROWS // bm
    s_tiles = S_ROWS // bm
    last = width_blocks_first - 1
    spec_p = pl.BlockSpec((bm, bk), lambda i, k: (jnp.minimum(i, p_tiles - 1), jnp.minimum(k, last)))
    spec_s = pl.BlockSpec(
        (bm, bk), lambda i, k: (jnp.clip(i - p_tiles, 0, s_tiles - 1), jnp.minimum(k, last)))
    return spec_p, spec_s, p_tiles


def kernel(x_prompt, x_sample, state_gla, c, c_ctx, norm_g, w_mod, b_mod, w_in_a, w_gate_up, b_gate, gla_norm_g, sg_norm_g, sg_norm_b, w_spatial, b_spatial, w_out_a, w_pw1, b_pw1, w_dw, b_dw, conv_norm_g, conv_norm_b, w_pw2, b_pw2, w_ffn_in, w_ffn_out):
    x = jnp.concatenate([x_prompt.reshape(P_ROWS, D), x_sample.reshape(S_ROWS, D)], axis=0)
    cv = jnp.concatenate([c_ctx[None, :], c, jnp.zeros((MOD_ROWS - 1 - DEC_BATCH, D), F32)], axis=0)
    mod = _modulation(cv, w_mod, b_mod).reshape(DEPTH * MOD_ROWS * N_MOD, 1, D)

    h = _first_norm(x, norm_g[0, 0].reshape(1, D), mod)
    new_states = []
    bm = BM_OUT
    for layer in range(DEPTH):
        g = norm_g[layer]
        i = layer // 2
        mid_norm = (g[2], (layer, 3, 4))
        if layer % 2 == 0:
            w_in = w_in_a[i]
            n_z1 = 2 * QK_W + 2 * V_W + 2 * RANK
            w_z1 = jnp.pad(w_in[:, :n_z1], ((0, 0), (0, Z1_W - n_z1))).astype(BF16)
            w_sg = w_in[:, n_z1:].astype(BF16)
            z1 = _matmul(h, w_z1, 640, F32)
            sg_out = _spatial_gating(h, w_sg, sg_norm_g[i], sg_norm_b[i], w_spatial[i], b_spatial[i])

            wf = w_gate_up[i, 0].reshape(RANK, HEADS, DK).transpose(1, 0, 2)
            wb = w_gate_up[i, 1].reshape(RANK, HEADS, DK).transpose(1, 0, 2)
            wg = jnp.zeros((HEADS, DK, 2 * DK), F32)
            wg = wg.at[:, 0:RANK, 0:DK].set(wf).at[:, RANK:2 * RANK, DK:2 * DK].set(wb).astype(BF16)
            bg = jnp.concatenate([b_gate[i, 0].reshape(HEADS, 1, DK), b_gate[i, 1].reshape(HEADS, 1, DK)],
                                 axis=-1)
            o_p, s_new = _gla(z1, wg, bg, gla_norm_g[i], SEQ, 0, BATCH, emit_state=True)
            (o_s,) = _gla(z1, wg, bg, gla_norm_g[i], DEC_SEQ, P_ROWS // DEC_SEQ, DEC_BATCH,
                          s0=state_gla, layer_idx=i)
            new_states.append(s_new)

            bk = 512
            spec_p, spec_s, p_tiles = _split_row_specs(bm, V_W // bk, bk)
            n_first = V_W // bk
            spec_sg = pl.BlockSpec((bm, bk), lambda r, k: (r, jnp.maximum(k - n_first, 0)))
            x, h = _out_proj([o_p, o_s, sg_out], [spec_p, spec_s, spec_sg], w_out_a[i].astype(BF16), bk,
                             x, mod, g[1], (layer, 2), mid_norm, split_rows=(n_first, p_tiles))
        else:
            z = _pw1_glu(h, w_pw1[i].astype(BF16), b_pw1[i])
            conv_p = _conv_rows(z, w_dw[i], b_dw[i], 0, P_ROWS, SEQ, 128)
            if i % 2 == 1:
                conv_s = _conv_cols(z, w_dw[i], b_dw[i], P_ROWS, S_ROWS, 128)
            else:
                conv_s = _conv_rows(z, w_dw[i], b_dw[i], P_ROWS, S_ROWS, GRID_W, 512)
            bm_pw2 = 256
            spec_p, spec_s, p_tiles = _split_row_specs(bm_pw2, 1, D)
            x, h = _out_proj([conv_p, conv_s], [spec_p, spec_s], w_pw2[i].astype(BF16), D,
                             x, mod, g[1], (layer, 2), mid_norm, split_rows=(1, p_tiles),
                             ln=(conv_norm_g[i], conv_norm_b[i]), bias=b_pw2[i], bm=bm_pw2)

        act = _ffn_in(h, w_ffn_in[layer].astype(BF16))
        bk = 512
        next_norm = None if layer == DEPTH - 1 else (norm_g[layer + 1, 0], (layer + 1, 0, 1))
        x, h = _out_proj([act], [pl.BlockSpec((bm, bk), lambda r, k: (r, k))],
                         w_ffn_out[layer].astype(BF16), bk, x, mod, g[3], (layer, 5), next_norm)

    y_prompt = x[:P_ROWS].reshape(BATCH, SEQ, D)
    y_sample = x[P_ROWS:].reshape(DEC_BATCH, DEC_SEQ, D)
    state_new = jnp.stack(new_states, axis=1).astype(x_prompt.dtype)
    return (y_prompt, y_sample, state_new)
```

```python
import functools

import jax
import jax.numpy as jnp
from jax import lax
from jax.experimental import pallas as pl
from jax.experimental.pallas import tpu as pltpu

F32 = jnp.float32
BF16 = jnp.bfloat16

D = 2048
BATCH = 16
SEQ = 256
DEPTH = 4
DEC_BATCH = 4
DEC_SEQ = 2048
GRID_W = 64
HEADS = 4
DK = 128
DV = 256
RANK = 16
TAU = 16.0
CHUNK = 64
SG_GROUPS = 4
SG_CH = 256
SG_CHUNK = 128
QK_W = HEADS * DK
V_W = HEADS * DV
SG_W = SG_GROUPS * SG_CH
CONV_K = 31
HALF_K = CONV_K // 2
D_FF = 5632
N_MOD = 6
EPS = 1e-6

P_ROWS = BATCH * SEQ
S_ROWS = DEC_BATCH * DEC_SEQ
ROWS = P_ROWS + S_ROWS
MOD_ROWS = 8
Z1_W = 2 * QK_W + 2 * V_W + 128

LANES = 128
SUBLANES = 8
BM_IN = 1024
BM_OUT = 512
EPI_ROWS = 32
GLA_ROWS = 4 * CHUNK
CONV_ROWS = 256
CONV_HALO = 16
VMEM_LIMIT = 56 * 1024 * 1024


def _params(sem):
    return pltpu.CompilerParams(dimension_semantics=sem, vmem_limit_bytes=VMEM_LIMIT)


def _dot(a, b):
    return jnp.dot(a, b, preferred_element_type=F32)


def _dot_nt(a, b):
    return lax.dot_general(a, b, (((1,), (1,)), ((), ())), preferred_element_type=F32)


def _dot_tn(a, b):
    return lax.dot_general(a, b, (((0,), (0,)), ((), ())), preferred_element_type=F32)


def _silu(x):
    return x / (1.0 + jnp.exp(-x))


def _rms(x, g):
    return x * lax.rsqrt(jnp.mean(x * x, axis=-1, keepdims=True) + EPS) * g


def _layer_norm(x, g, b):
    mu = jnp.mean(x, axis=-1, keepdims=True)
    xc = x - mu
    var = jnp.mean(xc * xc, axis=-1, keepdims=True)
    return xc * lax.rsqrt(var + EPS) * g + b


def _mod_row(i, bm):
    row0 = i * bm
    return jnp.where(row0 < P_ROWS, 0, 1 + (row0 - P_ROWS) // DEC_SEQ)


def _mod_spec(layer, chunk, bm):
    def index(i, *_):
        return ((layer * MOD_ROWS + _mod_row(i, bm)) * N_MOD + chunk, 0, 0)
    return pl.BlockSpec((1, 1, D), index)


def _row_vec_spec():
    return pl.BlockSpec((1, D), lambda i, *_: (0, 0))


def _mod_kernel(cv_ref, w_ref, b_ref, o_ref):
    s = _silu(cv_ref[...]).astype(BF16)
    o_ref[0] = _dot(s, w_ref[0].astype(BF16)) + b_ref[0]


def _modulation(cv, w_mod, b_mod):
    bn = 1024
    n = N_MOD * D
    return pl.pallas_call(
        _mod_kernel,
        grid=(DEPTH, n // bn),
        in_specs=[
            pl.BlockSpec((MOD_ROWS, D), lambda l, j: (0, 0)),
            pl.BlockSpec((1, D, bn), lambda l, j: (l, 0, j)),
            pl.BlockSpec((1, 1, bn), lambda l, j: (l, 0, j)),
        ],
        out_specs=pl.BlockSpec((1, MOD_ROWS, bn), lambda l, j: (l, 0, j)),
        out_shape=jax.ShapeDtypeStruct((DEPTH, MOD_ROWS, n), F32),
        name="modulation",
        compiler_params=_params(("arbitrary", "arbitrary")),
    )(cv, w_mod, b_mod.reshape(DEPTH, 1, n))


def _norm_kernel(xp_ref, xs_ref, g_ref, sh_ref, sc_ref, x_ref, h_ref, *, p_tiles):
    def emit(src_ref):
        x = src_ref[...]
        x_ref[...] = x
        h = _rms(x, g_ref[...]) * (1.0 + sc_ref[0]) + sh_ref[0]
        h_ref[...] = h.astype(BF16)

    i = pl.program_id(0)
    pl.when(i < p_tiles)(lambda: emit(xp_ref))
    pl.when(i >= p_tiles)(lambda: emit(xs_ref))


def _first_norm(x_prompt, x_sample, g, mod):
    bm = 256
    p_tiles = P_ROWS // bm
    s_tiles = S_ROWS // bm
    return pl.pallas_call(
        functools.partial(_norm_kernel, p_tiles=p_tiles),
        grid=(ROWS // bm,),
        in_specs=[
            pl.BlockSpec((bm, D), lambda i: (jnp.minimum(i, p_tiles - 1), 0)),
            pl.BlockSpec((bm, D), lambda i: (jnp.clip(i - p_tiles, 0, s_tiles - 1), 0)),
            _row_vec_spec(),
            _mod_spec(0, 0, bm),
            _mod_spec(0, 1, bm),
        ],
        out_specs=[pl.BlockSpec((bm, D), lambda i: (i, 0)), pl.BlockSpec((bm, D), lambda i: (i, 0))],
        out_shape=[jax.ShapeDtypeStruct((ROWS, D), F32), jax.ShapeDtypeStruct((ROWS, D), BF16)],
        name="first_norm",
        compiler_params=_params(("arbitrary",)),
    )(x_prompt, x_sample, g, mod, mod)


def _mm_kernel(h_ref, w_ref, o_ref):
    o_ref[...] = _dot(h_ref[...], w_ref[...]).astype(o_ref.dtype)


def _matmul(h, w, bn, out_dtype):
    bm = BM_IN
    n = w.shape[1]
    return pl.pallas_call(
        _mm_kernel,
        grid=(ROWS // bm, n // bn),
        in_specs=[
            pl.BlockSpec((bm, D), lambda i, j: (i, 0)),
            pl.BlockSpec((D, bn), lambda i, j: (0, j)),
        ],
        out_specs=pl.BlockSpec((bm, bn), lambda i, j: (i, j)),
        out_shape=jax.ShapeDtypeStruct((ROWS, n), out_dtype),
        name="in_proj",
        compiler_params=_params(("arbitrary", "arbitrary")),
    )(h, w)


def _swiglu_kernel(h_ref, wa_ref, wb_ref, o_ref):
    h = h_ref[...]
    a = _dot(h, wa_ref[...])
    b = _dot(h, wb_ref[...])
    o_ref[...] = (_silu(a) * b).astype(o_ref.dtype)


def _ffn_in(h, w):
    bm, bn = BM_IN, 512
    nb = D_FF // bn
    return pl.pallas_call(
        _swiglu_kernel,
        grid=(ROWS // bm, nb),
        in_specs=[
            pl.BlockSpec((bm, D), lambda i, j: (i, 0)),
            pl.BlockSpec((D, bn), lambda i, j: (0, j)),
            pl.BlockSpec((D, bn), lambda i, j: (0, j + nb)),
        ],
        out_specs=pl.BlockSpec((bm, bn), lambda i, j: (i, j)),
        out_shape=jax.ShapeDtypeStruct((ROWS, D_FF), BF16),
        name="ffn_in",
        compiler_params=_params(("arbitrary", "arbitrary")),
    )(h, w, w)


def _glu_kernel(h_ref, wa_ref, wb_ref, ba_ref, bb_ref, o_ref):
    h = h_ref[...]
    a = _dot(h, wa_ref[...]) + ba_ref[...]
    gt = _dot(h, wb_ref[...]) + bb_ref[...]
    o_ref[...] = a * (1.0 / (1.0 + jnp.exp(-gt)))


def _pw1_glu(h, w, b):
    bm, bn = BM_IN, 512
    nb = D // bn
    b2 = b.reshape(1, 2 * D)
    return pl.pallas_call(
        _glu_kernel,
        grid=(ROWS // bm, nb),
        in_specs=[
            pl.BlockSpec((bm, D), lambda i, j: (i, 0)),
            pl.BlockSpec((D, bn), lambda i, j: (0, j)),
            pl.BlockSpec((D, bn), lambda i, j: (0, j + nb)),
            pl.BlockSpec((1, bn), lambda i, j: (0, j)),
            pl.BlockSpec((1, bn), lambda i, j: (0, j + nb)),
        ],
        out_specs=pl.BlockSpec((bm, bn), lambda i, j: (i, j)),
        out_shape=jax.ShapeDtypeStruct((ROWS, D), F32),
        name="pw1_glu",
        compiler_params=_params(("arbitrary", "arbitrary")),
    )(h, w, w, b2, b2)


def _gelu_tanh(x):
    return 0.5 * x * (1.0 + jnp.tanh(0.7978845608028654 * (x + 0.044715 * (x * x * x))))


def _sg_kernel(h_ref, wu_ref, wv_ref, g_ref, b_ref, wsp_ref, bsp_ref, o_ref):
    h = h_ref[...]
    u = _gelu_tanh(_dot(h, wu_ref[...]))
    vv = _gelu_tanh(_dot(h, wv_ref[...]))
    vv = _layer_norm(vv, g_ref[0], b_ref[0]).astype(BF16)
    wsp = wsp_ref[0]
    bsp = bsp_ref[0]
    for c in range(h.shape[0] // SG_CHUNK):
        rows = slice(c * SG_CHUNK, (c + 1) * SG_CHUNK)
        mixed = _dot(wsp, vv[rows]) + bsp
        o_ref[rows, :] = (u[rows] * mixed).astype(o_ref.dtype)


def _spatial_gating(h, w_sg, sg_g, sg_b, w_sp, b_sp):
    bm = BM_IN
    return pl.pallas_call(
        _sg_kernel,
        grid=(ROWS // bm, SG_GROUPS),
        in_specs=[
            pl.BlockSpec((bm, D), lambda i, g: (i, 0)),
            pl.BlockSpec((D, SG_CH), lambda i, g: (0, g)),
            pl.BlockSpec((D, SG_CH), lambda i, g: (0, g + SG_GROUPS)),
            pl.BlockSpec((1, 1, SG_CH), lambda i, g: (g, 0, 0)),
            pl.BlockSpec((1, 1, SG_CH), lambda i, g: (g, 0, 0)),
            pl.BlockSpec((1, SG_CHUNK, SG_CHUNK), lambda i, g: (g, 0, 0)),
            pl.BlockSpec((1, SG_CHUNK, 1), lambda i, g: (g, 0, 0)),
        ],
        out_specs=pl.BlockSpec((bm, SG_CH), lambda i, g: (i, g)),
        out_shape=jax.ShapeDtypeStruct((ROWS, SG_W), BF16),
        name="spatial_gating",
        compiler_params=_params(("arbitrary", "arbitrary")),
    )(h, w_sg, w_sg, sg_g.reshape(SG_GROUPS, 1, SG_CH), sg_b.reshape(SG_GROUPS, 1, SG_CH),
      w_sp.astype(BF16), b_sp.reshape(SG_GROUPS, SG_CHUNK, 1))


def _chunk_triangles():
    i = jnp.arange(GLA_ROWS)[:, None]
    j = jnp.arange(GLA_ROWS)[None, :]
    same = (i // CHUNK) == (j // CHUNK)
    return jnp.stack([same & (j <= i), same & (j >= i)]).astype(BF16)


def _gla_kernel(*refs, seq, has_init, emit_state):
    q_ref, k_ref, v_ref, r_ref, g_ref, wg_ref, bg_ref, gg_ref, tri_ref = refs[:9]
    pos = 9
    s0_ref = None
    if has_init:
        s0_ref = refs[pos]
        pos += 1
    o_ref = refs[pos]
    pos += 1
    sfin_ref = None
    if emit_state:
        sfin_ref = refs[pos]
        pos += 1
    b_scr, of_scr, ob_scr = refs[pos:pos + 3]

    n_steps = seq // GLA_ROWS
    n_sub = GLA_ROWS // CHUNK
    scale = DK ** -0.5

    def decay_step(s, carry):
        rows = pl.ds(pl.multiple_of(s * GLA_ROWS, GLA_ROWS), GLA_ROWS)
        logits = _dot(g_ref[rows, :].astype(BF16), wg_ref[0]) + bg_ref[0]
        la = (jnp.minimum(logits, 0.0) - jnp.log1p(jnp.exp(-jnp.abs(logits)))) / TAU
        hi = la.astype(BF16)
        lo = (la - hi.astype(F32)).astype(BF16)
        tri = tri_ref[0]
        pre = _dot(tri, hi) + _dot(tri, lo)
        b_scr[rows, 0:DK] = pre[:, 0:DK]
        for c in range(n_sub):
            cs = slice(c * CHUNK, (c + 1) * CHUNK)
            total = pre[(c + 1) * CHUNK - 1:(c + 1) * CHUNK, DK:]
            b_scr[pl.ds(s * GLA_ROWS + c * CHUNK, CHUNK), DK:2 * DK] = total - pre[cs, DK:] + la[cs, DK:]
        return carry

    lax.fori_loop(0, n_steps, decay_step, 0)

    def rows_step(r0, direction, st, out_scr):
        rows = pl.ds(r0, GLA_ROWS)
        b = b_scr[rows, direction * DK:(direction + 1) * DK]
        last = [(c + 1) * CHUNK - 1 if direction == 0 else c * CHUNK for c in range(n_sub)]
        rem = jnp.concatenate(
            [b[last[c]:last[c] + 1, :] - b[c * CHUNK:(c + 1) * CHUNK, :] for c in range(n_sub)], axis=0)
        q = q_ref[rows, :] * scale
        k = k_ref[rows, :]
        v = v_ref[rows, :].astype(BF16)
        q_in = (q * jnp.exp(b)).astype(BF16)
        k_in = (k * jnp.exp(-b)).astype(BF16)
        k_out = (k * jnp.exp(rem)).astype(BF16)
        mask = tri_ref[direction] > 0
        att = jnp.where(mask, _dot_nt(q_in, k_in), 0.0).astype(BF16)
        o_intra = _dot(att, v)
        order = range(n_sub) if direction == 0 else range(n_sub - 1, -1, -1)
        for c in order:
            cs = slice(c * CHUNK, (c + 1) * CHUNK)
            out_scr[pl.ds(r0 + c * CHUNK, CHUNK), :] = o_intra[cs] + _dot_nt(q_in[cs], st.astype(BF16))
            st = st * jnp.exp(b[last[c]:last[c] + 1, :]) + _dot_tn(v[cs], k_out[cs])
        return st

    def body(s, carry):
        st_f, st_b = carry
        rf = pl.multiple_of(s * GLA_ROWS, GLA_ROWS)
        rb = pl.multiple_of((n_steps - 1 - s) * GLA_ROWS, GLA_ROWS)
        st_f = rows_step(rf, 0, st_f, of_scr)
        st_b = rows_step(rb, 1, st_b, ob_scr)
        return st_f, st_b

    if has_init:
        init = (s0_ref[0].T, s0_ref[1].T)
    else:
        init = (jnp.zeros((DV, DK), F32), jnp.zeros((DV, DK), F32))
    st_f, st_b = lax.fori_loop(0, n_steps, body, init)

    if emit_state:
        sfin_ref[0] = st_f.T
        sfin_ref[1] = st_b.T

    gg = gg_ref[0]

    def finish(i, carry):
        rows = pl.ds(pl.multiple_of(i * GLA_ROWS, GLA_ROWS), GLA_ROWS)
        o = _rms(of_scr[rows, :] + ob_scr[rows, :], gg)
        o_ref[rows, :] = (o * _silu(r_ref[rows, :])).astype(o_ref.dtype)
        return carry

    lax.fori_loop(0, n_steps, finish, 0)


def _gla(z1, wg, bg, gla_g, tri, seq, row_block0, n_seq, s0=None, layer_idx=0, emit_state=False):
    qk_blocks = QK_W // DK
    v_block0 = 2 * QK_W // DV
    r_block0 = v_block0 + HEADS
    g_block = (2 * QK_W + 2 * V_W) // DK
    rb = row_block0
    in_specs = [
        pl.BlockSpec((seq, DK), lambda b, h: (rb + b, h)),
        pl.BlockSpec((seq, DK), lambda b, h: (rb + b, qk_blocks + h)),
        pl.BlockSpec((seq, DV), lambda b, h: (rb + b, v_block0 + h)),
        pl.BlockSpec((seq, DV), lambda b, h: (rb + b, r_block0 + h)),
        pl.BlockSpec((seq, DK), lambda b, h: (rb + b, g_block)),
        pl.BlockSpec((1, DK, 2 * DK), lambda b, h: (h, 0, 0)),
        pl.BlockSpec((1, 1, 2 * DK), lambda b, h: (h, 0, 0)),
        pl.BlockSpec((1, 1, DV), lambda b, h: (h, 0, 0)),
        pl.BlockSpec((2, GLA_ROWS, GLA_ROWS), lambda b, h: (0, 0, 0)),
    ]
    args = [z1, z1, z1, z1, z1, wg, bg, gla_g.reshape(HEADS, 1, DV), tri]
    if s0 is not None:
        in_specs.append(pl.BlockSpec((None, None, 2, None, DK, DV),
                                     lambda b, h: (b, layer_idx, 0, h, 0, 0)))
        args.append(s0)
    out_specs = [pl.BlockSpec((seq, DV), lambda b, h: (b, h))]
    out_shape = [jax.ShapeDtypeStruct((n_seq * seq, V_W), BF16)]
    if emit_state:
        out_specs.append(pl.BlockSpec((None, 2, None, DK, DV), lambda b, h: (b, 0, h, 0, 0)))
        out_shape.append(jax.ShapeDtypeStruct((n_seq, 2, HEADS, DK, DV), F32))
    kern = functools.partial(_gla_kernel, seq=seq, has_init=s0 is not None, emit_state=emit_state)
    return pl.pallas_call(
        kern,
        grid=(n_seq, HEADS),
        in_specs=in_specs,
        out_specs=out_specs,
        out_shape=out_shape,
        scratch_shapes=[
            pltpu.VMEM((seq, 2 * DK), F32),
            pltpu.VMEM((seq, DV), F32),
            pltpu.VMEM((seq, DV), F32),
        ],
        name="gla_seq%d" % seq,
        compiler_params=_params(("arbitrary", "arbitrary")),
    )(*args)


def _ln_silu_rows(src_ref, g_ref, b_ref, o_ref):
    def step(r, carry):
        rows = pl.ds(pl.multiple_of(r * EPI_ROWS, EPI_ROWS), EPI_ROWS)
        y = _layer_norm(src_ref[rows, :], g_ref[...], b_ref[...])
        o_ref[rows, :] = _silu(y).astype(o_ref.dtype)
        return carry

    lax.fori_loop(0, o_ref.shape[0] // EPI_ROWS, step, 0, unroll=2)


def _conv_rows_kernel(z_ref, w_ref, b_ref, lg_ref, lb_ref, o_ref, pad_scr, conv_scr, *, seg):
    rows = z_ref.shape[0]
    n_seg = rows // seg
    plen = seg + 2 * CONV_HALO
    zeros = jnp.zeros((CONV_HALO, LANES), F32)
    first = CONV_HALO - HALF_K
    for ct in range(D // LANES):
        cols = slice(ct * LANES, (ct + 1) * LANES)
        for s in range(n_seg):
            base = s * plen
            pad_scr[ct, base:base + CONV_HALO, :] = zeros
            pad_scr[ct, base + CONV_HALO:base + CONV_HALO + seg, :] = z_ref[s * seg:(s + 1) * seg, cols]
            pad_scr[ct, base + CONV_HALO + seg:base + plen, :] = zeros
        acc = [jnp.zeros((seg, LANES), F32) + b_ref[:, cols] for _ in range(n_seg)]
        for t in range(CONV_K):
            tap = w_ref[t:t + 1, cols]
            for s in range(n_seg):
                start = s * plen + first + t
                acc[s] = acc[s] + pad_scr[ct, start:start + seg, :] * tap
        for s in range(n_seg):
            conv_scr[s * seg:(s + 1) * seg, cols] = acc[s]

    _ln_silu_rows(conv_scr, lg_ref, lb_ref, o_ref)


def _conv_rows(z, w, b, ln_g, ln_b, row0, n_rows, seg):
    rows = CONV_ROWS
    rb0 = row0 // rows
    plen = seg + 2 * CONV_HALO
    return pl.pallas_call(
        functools.partial(_conv_rows_kernel, seg=seg),
        grid=(n_rows // rows,),
        in_specs=[
            pl.BlockSpec((rows, D), lambda i: (rb0 + i, 0)),
            pl.BlockSpec((CONV_K, D), lambda i: (0, 0)),
            _row_vec_spec(),
            _row_vec_spec(),
            _row_vec_spec(),
        ],
        out_specs=pl.BlockSpec((rows, D), lambda i: (i, 0)),
        out_shape=jax.ShapeDtypeStruct((n_rows, D), BF16),
        scratch_shapes=[
            pltpu.VMEM((D // LANES, (rows // seg) * plen, LANES), F32),
            pltpu.VMEM((rows, D), F32),
        ],
        name="conv_rows_seg%d" % seg,
        compiler_params=_params(("arbitrary",)),
    )(z, w, b.reshape(1, D), ln_g.reshape(1, D), ln_b.reshape(1, D))


def _conv_cols_kernel(z_ref, w_ref, b_ref, o_ref, pad_scr):
    rows, ct = z_ref.shape
    halo = HALF_K * GRID_W
    zeros = jnp.zeros((halo, ct), F32)
    pad_scr[0:halo, :] = zeros
    pad_scr[halo:halo + rows, :] = z_ref[...]
    pad_scr[halo + rows:2 * halo + rows, :] = zeros
    blk = 256

    def body(i, carry):
        r0 = pl.multiple_of(i * blk, blk)
        acc = jnp.zeros((blk, ct), F32) + b_ref[...]
        for t in range(CONV_K):
            acc = acc + pad_scr[pl.ds(r0 + t * GRID_W, blk), :] * w_ref[t:t + 1, :]
        o_ref[pl.ds(r0, blk), :] = acc
        return carry

    lax.fori_loop(0, rows // blk, body, 0)


def _conv_cols(z, w, b, row0, n_rows, ct):
    rows = DEC_SEQ
    rb0 = row0 // rows
    return pl.pallas_call(
        _conv_cols_kernel,
        grid=(n_rows // rows, D // ct),
        in_specs=[
            pl.BlockSpec((rows, ct), lambda i, j: (rb0 + i, j)),
            pl.BlockSpec((CONV_K, ct), lambda i, j: (0, j)),
            pl.BlockSpec((1, ct), lambda i, j: (0, j)),
        ],
        out_specs=pl.BlockSpec((rows, ct), lambda i, j: (i, j)),
        out_shape=jax.ShapeDtypeStruct((n_rows, D), F32),
        scratch_shapes=[pltpu.VMEM((rows + 2 * HALF_K * GRID_W, ct), F32)],
        name="conv_cols",
        compiler_params=_params(("arbitrary", "arbitrary")),
    )(z, w, b.reshape(1, D))


def _ln_silu_kernel(y_ref, g_ref, b_ref, o_ref):
    _ln_silu_rows(y_ref, g_ref, b_ref, o_ref)


def _ln_silu(y, ln_g, ln_b):
    bm = 512
    n_rows = y.shape[0]
    return pl.pallas_call(
        _ln_silu_kernel,
        grid=(n_rows // bm,),
        in_specs=[pl.BlockSpec((bm, D), lambda i: (i, 0)), _row_vec_spec(), _row_vec_spec()],
        out_specs=pl.BlockSpec((bm, D), lambda i: (i, 0)),
        out_shape=jax.ShapeDtypeStruct((n_rows, D), BF16),
        name="ln_silu",
        compiler_params=_params(("arbitrary",)),
    )(y, ln_g.reshape(1, D), ln_b.reshape(1, D))


def _out_kernel(*refs, terms, nj, bn, p_tiles, has_bias, emit_h, split_out):
    pos = 0
    lhs = []
    for split in terms:
        n = 2 if split else 1
        lhs.append(refs[pos:pos + n])
        pos += n
    w_refs = refs[pos:pos + len(terms)]
    pos += len(terms)
    if has_bias:
        bias_ref = refs[pos]
        pos += 1
    x_ref, ga_ref, gate_ref = refs[pos:pos + 3]
    pos += 3
    if emit_h:
        gb_ref, sh_ref, sc_ref = refs[pos:pos + 3]
        pos += 3
    n_out = 2 if split_out else 1
    xo_refs = refs[pos:pos + n_out]
    pos += n_out
    if emit_h:
        h_ref = refs[pos]
        pos += 1
    y_scr, vec_scr = refs[pos:pos + 2]

    i = pl.program_id(0)
    j = pl.program_id(1)

    def column_tile(latent):
        y = None
        for l, w in zip(lhs, w_refs):
            a = l[1 if (latent and len(l) == 2) else 0][...]
            d = _dot(a, w[...])
            y = d if y is None else y + d
        y_scr[j] = y

    if any(terms):
        pl.when(i < p_tiles)(lambda: column_tile(False))
        pl.when(i >= p_tiles)(lambda: column_tile(True))
    else:
        column_tile(False)

    col_slices = [slice(jj * bn, (jj + 1) * bn) for jj in range(nj)]

    def finish(xo_ref):
        vec_scr[0] = gate_ref[0] * ga_ref[...]
        if emit_h:
            vec_scr[1] = gb_ref[...] * (1.0 + sc_ref[0])

        def rows_step(r, carry):
            rows = pl.ds(pl.multiple_of(r * EPI_ROWS, EPI_ROWS), EPI_ROWS)
            ys = []
            ssq = None
            for jj, cols in enumerate(col_slices):
                y = y_scr[jj, rows, :]
                if has_bias:
                    y = y + bias_ref[:, cols]
                ys.append(y)
                part = jnp.sum(y * y, axis=-1, keepdims=True)
                ssq = part if ssq is None else ssq + part
            inv = lax.rsqrt(ssq * (1.0 / D) + EPS)
            xs = []
            ssq = None
            for jj, cols in enumerate(col_slices):
                x_new = x_ref[rows, cols] + ys[jj] * inv * vec_scr[0, :, cols]
                xo_ref[rows, cols] = x_new
                if emit_h:
                    xs.append(x_new)
                    part = jnp.sum(x_new * x_new, axis=-1, keepdims=True)
                    ssq = part if ssq is None else ssq + part
            if emit_h:
                inv = lax.rsqrt(ssq * (1.0 / D) + EPS)
                for jj, cols in enumerate(col_slices):
                    h = xs[jj] * inv * vec_scr[1, :, cols] + sh_ref[0, :, cols]
                    h_ref[rows, cols] = h.astype(BF16)
            return carry

        lax.fori_loop(0, xo_ref.shape[0] // EPI_ROWS, rows_step, 0, unroll=2)

    last = j == nj - 1
    if split_out:
        pl.when(last & (i < p_tiles))(lambda: finish(xo_refs[0]))
        pl.when(last & (i >= p_tiles))(lambda: finish(xo_refs[1]))
    else:
        pl.when(last)(lambda: finish(xo_refs[0]))


def _out_proj(terms, w, x, mod, g_y, gate_idx, next_norm, *, bn, bias=None, split_out=False):
    bm = BM_OUT
    nj = D // bn
    p_tiles = P_ROWS // bm
    s_tiles = S_ROWS // bm
    emit_h = next_norm is not None
    in_specs, args = [], []
    for arrays, _, kt in terms:
        if len(arrays) == 2:
            in_specs += [
                pl.BlockSpec((bm, kt), lambda i, j: (jnp.minimum(i, p_tiles - 1), 0)),
                pl.BlockSpec((bm, kt), lambda i, j: (jnp.clip(i - p_tiles, 0, s_tiles - 1), 0)),
            ]
        else:
            in_specs.append(pl.BlockSpec((bm, kt), lambda i, j: (i, 0)))
        args += list(arrays)
    for _, k0, kt in terms:
        kb = k0 // kt
        if nj == 1:
            in_specs.append(pl.BlockSpec((kt, bn), lambda i, j, kb=kb: (kb, 0), pipeline_mode=pl.Buffered(1)))
        else:
            in_specs.append(pl.BlockSpec((kt, bn), lambda i, j, kb=kb: (kb, j)))
        args.append(w)
    if bias is not None:
        in_specs.append(_row_vec_spec())
        args.append(bias.reshape(1, D))
    in_specs += [pl.BlockSpec((bm, D), lambda i, j: (i, 0), pipeline_mode=pl.Buffered(1)),
                 _row_vec_spec(), _mod_spec(gate_idx[0], gate_idx[1], bm)]
    args += [x, g_y.reshape(1, D), mod]
    if split_out:
        out_specs = [pl.BlockSpec((bm, D), lambda i, j: (jnp.minimum(i, p_tiles - 1), 0)),
                     pl.BlockSpec((bm, D), lambda i, j: (jnp.clip(i - p_tiles, 0, s_tiles - 1), 0))]
        out_shape = [jax.ShapeDtypeStruct((P_ROWS, D), F32), jax.ShapeDtypeStruct((S_ROWS, D), F32)]
    else:
        out_specs = [pl.BlockSpec((bm, D), lambda i, j: (i, 0))]
        out_shape = [jax.ShapeDtypeStruct((ROWS, D), F32)]
    if emit_h:
        gain, (layer, shift_chunk, scale_chunk) = next_norm
        in_specs += [_row_vec_spec(), _mod_spec(layer, shift_chunk, bm), _mod_spec(layer, scale_chunk, bm)]
        args += [gain.reshape(1, D), mod, mod]
        out_specs.append(pl.BlockSpec((bm, D), lambda i, j: (i, 0)))
        out_shape.append(jax.ShapeDtypeStruct((ROWS, D), BF16))
    kern = functools.partial(_out_kernel, terms=tuple(len(t[0]) == 2 for t in terms), nj=nj, bn=bn,
                             p_tiles=p_tiles, has_bias=bias is not None, emit_h=emit_h, split_out=split_out)
    res = pl.pallas_call(
        kern,
        grid=(ROWS // bm, nj),
        in_specs=in_specs,
        out_specs=out_specs,
        out_shape=out_shape,
        scratch_shapes=[pltpu.VMEM((nj, bm, bn), F32), pltpu.VMEM((2, 1, D), F32)],
        name="out_proj_k%d" % w.shape[0],
        compiler_params=_params(("arbitrary", "arbitrary")),
    )(*args)
    if split_out:
        return (res[0], res[1]), None
    return (res[0], res[1]) if emit_h else (res[0], None)


def kernel(x_prompt, x_sample, state_gla, c, c_ctx, norm_g, w_mod, b_mod, w_in_a, w_gate_up, b_gate, gla_norm_g, sg_norm_g, sg_norm_b, w_spatial, b_spatial, w_out_a, w_pw1, b_pw1, w_dw, b_dw, conv_norm_g, conv_norm_b, w_pw2, b_pw2, w_ffn_in, w_ffn_out):
    cv =jnp.concatenate([c_ctx[None, :], c, jnp.zeros((MOD_ROWS - 1 - DEC_BATCH, D), F32)], axis=0)
    mod = _modulation(cv, w_mod, b_mod).reshape(DEPTH * MOD_ROWS * N_MOD, 1, D)
    tri = _chunk_triangles()

    x, h = _first_norm(x_prompt.reshape(P_ROWS, D), x_sample.reshape(S_ROWS, D),
                       norm_g[0, 0].reshape(1, D), mod)
    new_states = []
    for layer in range(DEPTH):
        g = norm_g[layer]
        i = layer // 2
        mid_norm = (g[2], (layer, 3, 4))
        if layer % 2 == 0:
            w_in = w_in_a[i]
            n_z1 = 2 * QK_W + 2 * V_W + 2 * RANK
            w_z1 = jnp.pad(w_in[:, :n_z1], ((0, 0), (0, Z1_W - n_z1))).astype(BF16)
            w_sg = w_in[:, n_z1:].astype(BF16)
            z1 = _matmul(h, w_z1, 640, F32)
            sg_out = _spatial_gating(h, w_sg, sg_norm_g[i], sg_norm_b[i], w_spatial[i], b_spatial[i])

            wf = w_gate_up[i, 0].reshape(RANK, HEADS, DK).transpose(1, 0, 2)
            wb = w_gate_up[i, 1].reshape(RANK, HEADS, DK).transpose(1, 0, 2)
            wg = jnp.zeros((HEADS, DK, 2 * DK), F32)
            wg = wg.at[:, 0:RANK, 0:DK].set(wf).at[:, RANK:2 * RANK, DK:2 * DK].set(wb).astype(BF16)
            bg = jnp.concatenate([b_gate[i, 0].reshape(HEADS, 1, DK), b_gate[i, 1].reshape(HEADS, 1, DK)],
                                 axis=-1)
            o_p, s_new = _gla(z1, wg, bg, gla_norm_g[i], tri, SEQ, 0, BATCH, emit_state=True)
            (o_s,) = _gla(z1, wg, bg, gla_norm_g[i], tri, DEC_SEQ, P_ROWS // DEC_SEQ, DEC_BATCH,
                          s0=state_gla, layer_idx=i)
            new_states.append(s_new)

            x, h = _out_proj([((o_p, o_s), 0, V_W), ((sg_out,), V_W, SG_W)], w_out_a[i].astype(BF16),
                             x, mod, g[1], (layer, 2), mid_norm, bn=D)
        else:
            z = _pw1_glu(h, w_pw1[i].astype(BF16), b_pw1[i])
            ln = (conv_norm_g[i], conv_norm_b[i])
            conv_p = _conv_rows(z, w_dw[i], b_dw[i], *ln, 0, P_ROWS, SEQ)
            if i % 2 == 1:
                conv_s = _ln_silu(_conv_cols(z, w_dw[i], b_dw[i], P_ROWS, S_ROWS, 128), *ln)
            else:
                conv_s = _conv_rows(z, w_dw[i], b_dw[i], *ln, P_ROWS, S_ROWS, GRID_W)
            x, h = _out_proj([((conv_p, conv_s), 0, D)], w_pw2[i].astype(BF16),
                             x, mod, g[1], (layer, 2), mid_norm, bn=D, bias=b_pw2[i])

        act = _ffn_in(h, w_ffn_in[layer].astype(BF16))
        final = layer == DEPTH - 1
        next_norm = None if final else (norm_g[layer + 1, 0], (layer + 1, 0, 1))
        x, h = _out_proj([((act,), 0, D_FF)], w_ffn_out[layer].astype(BF16),
                         x, mod, g[3], (layer, 5), next_norm, bn=512, split_out=final)

    y_prompt = x[0].reshape(BATCH, SEQ, D)
    y_sample = x[1].reshape(DEC_BATCH, DEC_SEQ, D)
    state_new = jnp.stack(new_states, axis=1).astype(x_prompt.dtype)
    return (y_prompt, y_sample, state_new)
```

```python
import functools

import jax
import jax.numpy as jnp
from jax import lax
from jax.experimental import pallas as pl
from jax.experimental.pallas import tpu as pltpu

F32 = jnp.float32
BF16 = jnp.bfloat16

D = 2048
BATCH = 16
SEQ = 256
DEPTH = 4
DEC_BATCH = 4
DEC_SEQ = 2048
GRID_W = 64
HEADS = 4
DK = 128
DV = 256
RANK = 16
TAU = 16.0
CHUNK = 64
SG_GROUPS = 4
SG_CH = 256
SG_CHUNK = 128
QK_W = HEADS * DK
V_W = HEADS * DV
SG_W = SG_GROUPS * SG_CH
CONV_K = 31
HALF_K = CONV_K // 2
D_FF = 5632
N_MOD = 6
EPS = 1e-6

P_ROWS = BATCH * SEQ
S_ROWS = DEC_BATCH * DEC_SEQ
ROWS = P_ROWS + S_ROWS
MOD_ROWS = 8
Z1_W = 2 * QK_W + 2 * V_W + 128

LANES = 128
SUBLANES = 8
BM_IN = 1024
BM_OUT = 512
FFN_OUT_BN = 512
EPI_ROWS = 32
GLA_ROWS = 4 * CHUNK
CONV_ROWS = 256
CONV_HALO = 16
VMEM_LIMIT = 56 * 1024 * 1024


def _params(sem):
    return pltpu.CompilerParams(dimension_semantics=sem, vmem_limit_bytes=VMEM_LIMIT)


def _dot(a, b):
    return jnp.dot(a, b, preferred_element_type=F32)


def _dot_nt(a, b):
    return lax.dot_general(a, b, (((1,), (1,)), ((), ())), preferred_element_type=F32)


def _dot_tn(a, b):
    return lax.dot_general(a, b, (((0,), (0,)), ((), ())), preferred_element_type=F32)


def _silu(x):
    return x / (1.0 + jnp.exp(-x))


def _rms(x, g):
    return x * lax.rsqrt(jnp.mean(x * x, axis=-1, keepdims=True) + EPS) * g


def _layer_norm(x, g, b):
    mu = jnp.mean(x, axis=-1, keepdims=True)
    xc = x - mu
    var = jnp.mean(xc * xc, axis=-1, keepdims=True)
    return xc * lax.rsqrt(var + EPS) * g + b


def _mod_row(i, bm):
    row0 = i * bm
    return jnp.where(row0 < P_ROWS, 0, 1 + (row0 - P_ROWS) // DEC_SEQ)


def _mod_spec(layer, chunk, bm):
    def index(i, *_):
        return ((layer * MOD_ROWS + _mod_row(i, bm)) * N_MOD + chunk, 0, 0)
    return pl.BlockSpec((1, 1, D), index)


def _row_vec_spec():
    return pl.BlockSpec((1, D), lambda i, *_: (0, 0))


def _mod_kernel(cv_ref, w_ref, b_ref, o_ref):
    s = _silu(cv_ref[...]).astype(BF16)
    o_ref[0] = _dot(s, w_ref[0].astype(BF16)) + b_ref[0]


def _modulation(cv, w_mod, b_mod):
    bn = 1024
    n = N_MOD * D
    return pl.pallas_call(
        _mod_kernel,
        grid=(DEPTH, n // bn),
        in_specs=[
            pl.BlockSpec((MOD_ROWS, D), lambda l, j: (0, 0)),
            pl.BlockSpec((1, D, bn), lambda l, j: (l, 0, j)),
            pl.BlockSpec((1, 1, bn), lambda l, j: (l, 0, j)),
        ],
        out_specs=pl.BlockSpec((1, MOD_ROWS, bn), lambda l, j: (l, 0, j)),
        out_shape=jax.ShapeDtypeStruct((DEPTH, MOD_ROWS, n), F32),
        name="modulation",
        compiler_params=_params(("arbitrary", "arbitrary")),
    )(cv, w_mod, b_mod.reshape(DEPTH, 1, n))


def _norm_kernel(xp_ref, xs_ref, g_ref, sh_ref, sc_ref, x_ref, h_ref, *, p_tiles):
    def emit(src_ref):
        x = src_ref[...]
        x_ref[...] = x
        h = _rms(x, g_ref[...]) * (1.0 + sc_ref[0]) + sh_ref[0]
        h_ref[...] = h.astype(BF16)

    i = pl.program_id(0)
    pl.when(i < p_tiles)(lambda: emit(xp_ref))
    pl.when(i >= p_tiles)(lambda: emit(xs_ref))


def _first_norm(x_prompt, x_sample, g, mod):
    bm = 256
    p_tiles = P_ROWS // bm
    s_tiles = S_ROWS // bm
    return pl.pallas_call(
        functools.partial(_norm_kernel, p_tiles=p_tiles),
        grid=(ROWS // bm,),
        in_specs=[
            pl.BlockSpec((bm, D), lambda i: (jnp.minimum(i, p_tiles - 1), 0)),
            pl.BlockSpec((bm, D), lambda i: (jnp.clip(i - p_tiles, 0, s_tiles - 1), 0)),
            _row_vec_spec(),
            _mod_spec(0, 0, bm),
            _mod_spec(0, 1, bm),
        ],
        out_specs=[pl.BlockSpec((bm, D), lambda i: (i, 0)), pl.BlockSpec((bm, D), lambda i: (i, 0))],
        out_shape=[jax.ShapeDtypeStruct((ROWS, D), F32), jax.ShapeDtypeStruct((ROWS, D), BF16)],
        name="first_norm",
        compiler_params=_params(("arbitrary",)),
    )(x_prompt, x_sample, g, mod, mod)


def _cast_weights_once(pairs):
    @pl.when(pl.program_id(1) == 0)
    def _():
        for w_ref, w_scr in pairs:
            w_scr[...] = w_ref[...].astype(BF16)


def _w_spec(layer, bn, block0=0):
    return pl.BlockSpec((None, D, bn), lambda j, i: (layer, 0, block0 + j))


def _mm_kernel(h_ref, w_ref, o_ref, w_scr):
    _cast_weights_once([(w_ref, w_scr)])
    o_ref[...] = _dot(h_ref[...], w_scr[...]).astype(o_ref.dtype)


def _matmul(h, w, layer, n, bn, out_dtype):
    bm = BM_IN
    return pl.pallas_call(
        _mm_kernel,
        grid=(n // bn, ROWS // bm),
        in_specs=[pl.BlockSpec((bm, D), lambda j, i: (i, 0)), _w_spec(layer, bn)],
        out_specs=pl.BlockSpec((bm, bn), lambda j, i: (i, j)),
        out_shape=jax.ShapeDtypeStruct((ROWS, n), out_dtype),
        scratch_shapes=[pltpu.VMEM((D, bn), BF16)],
        name="in_proj",
        compiler_params=_params(("arbitrary", "arbitrary")),
    )(h, w)


def _swiglu_kernel(h_ref, wa_ref, wb_ref, o_ref, wa_scr, wb_scr):
    _cast_weights_once([(wa_ref, wa_scr), (wb_ref, wb_scr)])
    h = h_ref[...]
    a = _dot(h, wa_scr[...])
    b = _dot(h, wb_scr[...])
    o_ref[...] = (_silu(a) * b).astype(o_ref.dtype)


def _ffn_in(h, w, layer):
    bm, bn = BM_IN, 512
    nb = D_FF // bn
    return pl.pallas_call(
        _swiglu_kernel,
        grid=(nb, ROWS // bm),
        in_specs=[pl.BlockSpec((bm, D), lambda j, i: (i, 0)), _w_spec(layer, bn), _w_spec(layer, bn, nb)],
        out_specs=pl.BlockSpec((bm, bn), lambda j, i: (i, j)),
        out_shape=jax.ShapeDtypeStruct((ROWS, D_FF), BF16),
        scratch_shapes=[pltpu.VMEM((D, bn), BF16), pltpu.VMEM((D, bn), BF16)],
        name="ffn_in",
        compiler_params=_params(("arbitrary", "arbitrary")),
    )(h, w, w)


def _glu_kernel(h_ref, wa_ref, wb_ref, ba_ref, bb_ref, o_ref, wa_scr, wb_scr):
    _cast_weights_once([(wa_ref, wa_scr), (wb_ref, wb_scr)])
    h = h_ref[...]
    a = _dot(h, wa_scr[...]) + ba_ref[...]
    gt = _dot(h, wb_scr[...]) + bb_ref[...]
    o_ref[...] = a * (1.0 / (1.0 + jnp.exp(-gt)))


def _pw1_glu(h, w, b, layer):
    bm, bn = BM_IN, 512
    nb = D // bn
    b2 = b.reshape(1, 2 * D)
    return pl.pallas_call(
        _glu_kernel,
        grid=(nb, ROWS // bm),
        in_specs=[
            pl.BlockSpec((bm, D), lambda j, i: (i, 0)),
            _w_spec(layer, bn),
            _w_spec(layer, bn, nb),
            pl.BlockSpec((1, bn), lambda j, i: (0, j)),
            pl.BlockSpec((1, bn), lambda j, i: (0, j + nb)),
        ],
        out_specs=pl.BlockSpec((bm, bn), lambda j, i: (i, j)),
        out_shape=jax.ShapeDtypeStruct((ROWS, D), F32),
        scratch_shapes=[pltpu.VMEM((D, bn), BF16), pltpu.VMEM((D, bn), BF16)],
        name="pw1_glu",
        compiler_params=_params(("arbitrary", "arbitrary")),
    )(h, w, w, b2, b2)


def _gelu_tanh(x):
    return 0.5 * x * (1.0 + jnp.tanh(0.7978845608028654 * (x + 0.044715 * (x * x * x))))


def _sg_kernel(h_ref, wu_ref, wv_ref, g_ref, b_ref, wsp_ref, bsp_ref, o_ref):
    h = h_ref[...]
    u = _gelu_tanh(_dot(h, wu_ref[...]))
    vv = _gelu_tanh(_dot(h, wv_ref[...]))
    vv = _layer_norm(vv, g_ref[0], b_ref[0]).astype(BF16)
    wsp = wsp_ref[0]
    bsp = bsp_ref[0]
    for c in range(h.shape[0] // SG_CHUNK):
        rows = slice(c * SG_CHUNK, (c + 1) * SG_CHUNK)
        mixed = _dot(wsp, vv[rows]) + bsp
        o_ref[rows, :] = (u[rows] * mixed).astype(o_ref.dtype)


def _spatial_gating(h, w_sg, sg_g, sg_b, w_sp, b_sp):
    bm = BM_IN
    return pl.pallas_call(
        _sg_kernel,
        grid=(ROWS // bm, SG_GROUPS),
        in_specs=[
            pl.BlockSpec((bm, D), lambda i, g: (i, 0)),
            pl.BlockSpec((D, SG_CH), lambda i, g: (0, g)),
            pl.BlockSpec((D, SG_CH), lambda i, g: (0, g + SG_GROUPS)),
            pl.BlockSpec((1, 1, SG_CH), lambda i, g: (g, 0, 0)),
            pl.BlockSpec((1, 1, SG_CH), lambda i, g: (g, 0, 0)),
            pl.BlockSpec((1, SG_CHUNK, SG_CHUNK), lambda i, g: (g, 0, 0)),
            pl.BlockSpec((1, SG_CHUNK, 1), lambda i, g: (g, 0, 0)),
        ],
        out_specs=pl.BlockSpec((bm, SG_CH), lambda i, g: (i, g)),
        out_shape=jax.ShapeDtypeStruct((ROWS, SG_W), BF16),
        name="spatial_gating",
        compiler_params=_params(("arbitrary", "arbitrary")),
    )(h, w_sg, w_sg, sg_g.reshape(SG_GROUPS, 1, SG_CH), sg_b.reshape(SG_GROUPS, 1, SG_CH),
      w_sp.astype(BF16), b_sp.reshape(SG_GROUPS, SG_CHUNK, 1))


def _chunk_triangles():
    i = jnp.arange(GLA_ROWS)[:, None]
    j = jnp.arange(GLA_ROWS)[None, :]
    same = (i // CHUNK) == (j // CHUNK)
    return jnp.stack([same & (j <= i), same & (j >= i)]).astype(BF16)


def _gla_kernel(*refs, seq, has_init, emit_state):
    q_ref, k_ref, v_ref, r_ref, g_ref, wg_ref, bg_ref, gg_ref, tri_ref = refs[:9]
    pos = 9
    s0_ref = None
    if has_init:
        s0_ref = refs[pos]
        pos += 1
    o_ref = refs[pos]
    pos += 1
    sfin_ref = None
    if emit_state:
        sfin_ref = refs[pos]
        pos += 1
    b_scr, of_scr, ob_scr = refs[pos:pos + 3]

    n_steps = seq // GLA_ROWS
    n_sub = GLA_ROWS // CHUNK
    scale = DK ** -0.5

    def decay_step(s, carry):
        rows = pl.ds(pl.multiple_of(s * GLA_ROWS, GLA_ROWS), GLA_ROWS)
        logits = _dot(g_ref[rows, :].astype(BF16), wg_ref[0]) + bg_ref[0]
        la = (jnp.minimum(logits, 0.0) - jnp.log1p(jnp.exp(-jnp.abs(logits)))) / TAU
        hi = la.astype(BF16)
        lo = (la - hi.astype(F32)).astype(BF16)
        tri = tri_ref[0]
        pre = _dot(tri, hi) + _dot(tri, lo)
        b_scr[rows, 0:DK] = pre[:, 0:DK]
        for c in range(n_sub):
            cs = slice(c * CHUNK, (c + 1) * CHUNK)
            total = pre[(c + 1) * CHUNK - 1:(c + 1) * CHUNK, DK:]
            b_scr[pl.ds(s * GLA_ROWS + c * CHUNK, CHUNK), DK:2 * DK] = total - pre[cs, DK:] + la[cs, DK:]
        return carry

    lax.fori_loop(0, n_steps, decay_step, 0)

    def rows_step(r0, direction, st, out_scr):
        rows = pl.ds(r0, GLA_ROWS)
        b = b_scr[rows, direction * DK:(direction + 1) * DK]
        last = [(c + 1) * CHUNK - 1 if direction == 0 else c * CHUNK for c in range(n_sub)]
        rem = jnp.concatenate(
            [b[last[c]:last[c] + 1, :] - b[c * CHUNK:(c + 1) * CHUNK, :] for c in range(n_sub)], axis=0)
        q = q_ref[rows, :] * scale
        k = k_ref[rows, :]
        v = v_ref[rows, :].astype(BF16)
        q_in = (q * jnp.exp(b)).astype(BF16)
        k_in = (k * jnp.exp(-b)).astype(BF16)
        k_out = (k * jnp.exp(rem)).astype(BF16)
        mask = tri_ref[direction] > 0
        att = jnp.where(mask, _dot_nt(q_in, k_in), 0.0).astype(BF16)
        o_intra = _dot(att, v)
        order = range(n_sub) if direction == 0 else range(n_sub - 1, -1, -1)
        for c in order:
            cs = slice(c * CHUNK, (c + 1) * CHUNK)
            out_scr[pl.ds(r0 + c * CHUNK, CHUNK), :] = o_intra[cs] + _dot_nt(q_in[cs], st.astype(BF16))
            st = st * jnp.exp(b[last[c]:last[c] + 1, :]) + _dot_tn(v[cs], k_out[cs])
        return st

    def body(s, carry):
        st_f, st_b = carry
        rf = pl.multiple_of(s * GLA_ROWS, GLA_ROWS)
        rb = pl.multiple_of((n_steps - 1 - s) * GLA_ROWS, GLA_ROWS)
        st_f = rows_step(rf, 0, st_f, of_scr)
        st_b = rows_step(rb, 1, st_b, ob_scr)
        return st_f, st_b

    if has_init:
        init = (s0_ref[0].T, s0_ref[1].T)
    else:
        init = (jnp.zeros((DV, DK), F32), jnp.zeros((DV, DK), F32))
    st_f, st_b = lax.fori_loop(0, n_steps, body, init)

    if emit_state:
        sfin_ref[0] = st_f.T
        sfin_ref[1] = st_b.T

    gg = gg_ref[0]

    def finish(i, carry):
        rows = pl.ds(pl.multiple_of(i * GLA_ROWS, GLA_ROWS), GLA_ROWS)
        o = _rms(of_scr[rows, :] + ob_scr[rows, :], gg)
        o_ref[rows, :] = (o * _silu(r_ref[rows, :])).astype(o_ref.dtype)
        return carry

    lax.fori_loop(0, n_steps, finish, 0)


def _gla(z1, wg, bg, gla_g, tri, seq, row_block0, n_seq, s0=None, layer_idx=0, emit_state=False):
    qk_blocks = QK_W // DK
    v_block0 = 2 * QK_W // DV
    r_block0 = v_block0 + HEADS
    g_block = (2 * QK_W + 2 * V_W) // DK
    rb = row_block0
    in_specs = [
        pl.BlockSpec((seq, DK), lambda b, h: (rb + b, h)),
        pl.BlockSpec((seq, DK), lambda b, h: (rb + b, qk_blocks + h)),
        pl.BlockSpec((seq, DV), lambda b, h: (rb + b, v_block0 + h)),
        pl.BlockSpec((seq, DV), lambda b, h: (rb + b, r_block0 + h)),
        pl.BlockSpec((seq, DK), lambda b, h: (rb + b, g_block)),
        pl.BlockSpec((1, DK, 2 * DK), lambda b, h: (h, 0, 0)),
        pl.BlockSpec((1, 1, 2 * DK), lambda b, h: (h, 0, 0)),
        pl.BlockSpec((1, 1, DV), lambda b, h: (h, 0, 0)),
        pl.BlockSpec((2, GLA_ROWS, GLA_ROWS), lambda b, h: (0, 0, 0)),
    ]
    args = [z1, z1, z1, z1, z1, wg, bg, gla_g.reshape(HEADS, 1, DV), tri]
    if s0 is not None:
        in_specs.append(pl.BlockSpec((None, None, 2, None, DK, DV),
                                     lambda b, h: (b, layer_idx, 0, h, 0, 0)))
        args.append(s0)
    out_specs = [pl.BlockSpec((seq, DV), lambda b, h: (b, h))]
    out_shape = [jax.ShapeDtypeStruct((n_seq * seq, V_W), BF16)]
    if emit_state:
        out_specs.append(pl.BlockSpec((None, 2, None, DK, DV), lambda b, h: (b, 0, h, 0, 0)))
        out_shape.append(jax.ShapeDtypeStruct((n_seq, 2, HEADS, DK, DV), F32))
    kern = functools.partial(_gla_kernel, seq=seq, has_init=s0 is not None, emit_state=emit_state)
    return pl.pallas_call(
        kern,
        grid=(n_seq, HEADS),
        in_specs=in_specs,
        out_specs=out_specs,
        out_shape=out_shape,
        scratch_shapes=[
            pltpu.VMEM((seq, 2 * DK), F32),
            pltpu.VMEM((seq, DV), F32),
            pltpu.VMEM((seq, DV), F32),
        ],
        name="gla_seq%d" % seq,
        compiler_params=_params(("arbitrary", "arbitrary")),
    )(*args)


def _ln_silu_rows(src_ref, g_ref, b_ref, o_ref):
    def step(r, carry):
        rows = pl.ds(pl.multiple_of(r * EPI_ROWS, EPI_ROWS), EPI_ROWS)
        y = _layer_norm(src_ref[rows, :], g_ref[...], b_ref[...])
        o_ref[rows, :] = _silu(y).astype(o_ref.dtype)
        return carry

    lax.fori_loop(0, o_ref.shape[0] // EPI_ROWS, step, 0, unroll=2)


def _conv_rows_kernel(z_ref, w_ref, b_ref, lg_ref, lb_ref, o_ref, pad_scr, conv_scr, *, seg):
    rows = z_ref.shape[0]
    n_seg = rows // seg
    plen = seg + 2 * CONV_HALO
    zeros = jnp.zeros((CONV_HALO, LANES), F32)
    first = CONV_HALO - HALF_K
    for ct in range(D // LANES):
        cols = slice(ct * LANES, (ct + 1) * LANES)
        for s in range(n_seg):
            base = s * plen
            pad_scr[ct, base:base + CONV_HALO, :] = zeros
            pad_scr[ct, base + CONV_HALO:base + CONV_HALO + seg, :] = z_ref[s * seg:(s + 1) * seg, cols]
            pad_scr[ct, base + CONV_HALO + seg:base + plen, :] = zeros
        acc = [jnp.zeros((seg, LANES), F32) + b_ref[:, cols] for _ in range(n_seg)]
        for t in range(CONV_K):
            tap = w_ref[t:t + 1, cols]
            for s in range(n_seg):
                start = s * plen + first + t
                acc[s] = acc[s] + pad_scr[ct, start:start + seg, :] * tap
        for s in range(n_seg):
            conv_scr[s * seg:(s + 1) * seg, cols] = acc[s]

    _ln_silu_rows(conv_scr, lg_ref, lb_ref, o_ref)


def _conv_rows(z, w, b, ln_g, ln_b, row0, n_rows, seg):
    rows = CONV_ROWS
    rb0 = row0 // rows
    plen = seg + 2 * CONV_HALO
    return pl.pallas_call(
        functools.partial(_conv_rows_kernel, seg=seg),
        grid=(n_rows // rows,),
        in_specs=[
            pl.BlockSpec((rows, D), lambda i: (rb0 + i, 0)),
            pl.BlockSpec((CONV_K, D), lambda i: (0, 0)),
            _row_vec_spec(),
            _row_vec_spec(),
            _row_vec_spec(),
        ],
        out_specs=pl.BlockSpec((rows, D), lambda i: (i, 0)),
        out_shape=jax.ShapeDtypeStruct((n_rows, D), BF16),
        scratch_shapes=[
            pltpu.VMEM((D // LANES, (rows // seg) * plen, LANES), F32),
            pltpu.VMEM((rows, D), F32),
        ],
        name="conv_rows_seg%d" % seg,
        compiler_params=_params(("arbitrary",)),
    )(z, w, b.reshape(1, D), ln_g.reshape(1, D), ln_b.reshape(1, D))


def _conv_cols_kernel(z_ref, w_ref, b_ref, o_ref, pad_scr):
    rows, ct = z_ref.shape
    halo = HALF_K * GRID_W
    zeros = jnp.zeros((halo, ct), F32)
    pad_scr[0:halo, :] = zeros
    pad_scr[halo:halo + rows, :] = z_ref[...]
    pad_scr[halo + rows:2 * halo + rows, :] = zeros
    blk = 256

    def body(i, carry):
        r0 = pl.multiple_of(i * blk, blk)
        acc = jnp.zeros((blk, ct), F32) + b_ref[...]
        for t in range(CONV_K):
            acc = acc + pad_scr[pl.ds(r0 + t * GRID_W, blk), :] * w_ref[t:t + 1, :]
        o_ref[pl.ds(r0, blk), :] = acc
        return carry

    lax.fori_loop(0, rows // blk, body, 0)


def _conv_cols(z, w, b, row0, n_rows, ct):
    rows = DEC_SEQ
    rb0 = row0 // rows
    return pl.pallas_call(
        _conv_cols_kernel,
        grid=(n_rows // rows, D // ct),
        in_specs=[
            pl.BlockSpec((rows, ct), lambda i, j: (rb0 + i, j)),
            pl.BlockSpec((CONV_K, ct), lambda i, j: (0, j)),
            pl.BlockSpec((1, ct), lambda i, j: (0, j)),
        ],
        out_specs=pl.BlockSpec((rows, ct), lambda i, j: (i, j)),
        out_shape=jax.ShapeDtypeStruct((n_rows, D), F32),
        scratch_shapes=[pltpu.VMEM((rows + 2 * HALF_K * GRID_W, ct), F32)],
        name="conv_cols",
        compiler_params=_params(("arbitrary", "arbitrary")),
    )(z, w, b.reshape(1, D))


def _ln_silu_kernel(y_ref, g_ref, b_ref, o_ref):
    _ln_silu_rows(y_ref, g_ref, b_ref, o_ref)


def _ln_silu(y, ln_g, ln_b):
    bm = 512
    n_rows = y.shape[0]
    return pl.pallas_call(
        _ln_silu_kernel,
        grid=(n_rows // bm,),
        in_specs=[pl.BlockSpec((bm, D), lambda i: (i, 0)), _row_vec_spec(), _row_vec_spec()],
        out_specs=pl.BlockSpec((bm, D), lambda i: (i, 0)),
        out_shape=jax.ShapeDtypeStruct((n_rows, D), BF16),
        name="ln_silu",
        compiler_params=_params(("arbitrary",)),
    )(y, ln_g.reshape(1, D), ln_b.reshape(1, D))


def _out_kernel(*refs, terms, nj, bn, p_tiles, has_bias, emit_h, split_out):
    pos = 0
    lhs = []
    for split in terms:
        n = 2 if split else 1
        lhs.append(refs[pos:pos + n])
        pos += n
    w_refs = refs[pos:pos + len(terms)]
    pos += len(terms)
    if has_bias:
        bias_ref = refs[pos]
        pos += 1
    x_ref, ga_ref, gate_ref = refs[pos:pos + 3]
    pos += 3
    if emit_h:
        gb_ref, sh_ref, sc_ref = refs[pos:pos + 3]
        pos += 3
    n_out = 2 if split_out else 1
    xo_refs = refs[pos:pos + n_out]
    pos += n_out
    if emit_h:
        h_ref = refs[pos]
        pos += 1
    y_scr, vec_scr = refs[pos:pos + 2]

    i = pl.program_id(0)
    j = pl.program_id(1)

    def column_tile(latent):
        y = None
        for l, w in zip(lhs, w_refs):
            a = l[1 if (latent and len(l) == 2) else 0][...]
            d = _dot(a, w[...])
            y = d if y is None else y + d
        y_scr[j] = y

    if any(terms):
        pl.when(i < p_tiles)(lambda: column_tile(False))
        pl.when(i >= p_tiles)(lambda: column_tile(True))
    else:
        column_tile(False)

    col_slices = [slice(jj * bn, (jj + 1) * bn) for jj in range(nj)]

    def finish(xo_ref):
        vec_scr[0] = gate_ref[0] * ga_ref[...]
        if emit_h:
            vec_scr[1] = gb_ref[...] * (1.0 + sc_ref[0])

        def rows_step(r, carry):
            rows = pl.ds(pl.multiple_of(r * EPI_ROWS, EPI_ROWS), EPI_ROWS)
            ys = []
            ssq = None
            for jj, cols in enumerate(col_slices):
                y = y_scr[jj, rows, :]
                if has_bias:
                    y = y + bias_ref[:, cols]
                ys.append(y)
                part = jnp.sum(y * y, axis=-1, keepdims=True)
                ssq = part if ssq is None else ssq + part
            inv = lax.rsqrt(ssq * (1.0 / D) + EPS)
            xs = []
            ssq = None
            for jj, cols in enumerate(col_slices):
                x_new = x_ref[rows, cols] + ys[jj] * inv * vec_scr[0, :, cols]
                xo_ref[rows, cols] = x_new
                if emit_h:
                    xs.append(x_new)
                    part = jnp.sum(x_new * x_new, axis=-1, keepdims=True)
                    ssq = part if ssq is None else ssq + part
            if emit_h:
                inv = lax.rsqrt(ssq * (1.0 / D) + EPS)
                for jj, cols in enumerate(col_slices):
                    h = xs[jj] * inv * vec_scr[1, :, cols] + sh_ref[0, :, cols]
                    h_ref[rows, cols] = h.astype(BF16)
            return carry

        lax.fori_loop(0, xo_ref.shape[0] // EPI_ROWS, rows_step, 0, unroll=2)

    last = j == nj - 1
    if split_out:
        pl.when(last & (i < p_tiles))(lambda: finish(xo_refs[0]))
        pl.when(last & (i >= p_tiles))(lambda: finish(xo_refs[1]))
    else:
        pl.when(last)(lambda: finish(xo_refs[0]))


def _out_proj(terms, w, layer_w, x, mod, g_y, gate_idx, next_norm, *, bn, bias=None, split_out=False):
    bm = BM_OUT
    nj = D // bn
    p_tiles = P_ROWS // bm
    s_tiles = S_ROWS // bm
    emit_h = next_norm is not None
    in_specs, args = [], []
    for arrays, _, kt in terms:
        if len(arrays) == 2:
            in_specs += [
                pl.BlockSpec((bm, kt), lambda i, j: (jnp.minimum(i, p_tiles - 1), 0)),
                pl.BlockSpec((bm, kt), lambda i, j: (jnp.clip(i - p_tiles, 0, s_tiles - 1), 0)),
            ]
        else:
            in_specs.append(pl.BlockSpec((bm, kt), lambda i, j: (i, 0)))
        args += list(arrays)
    for _, k0, kt in terms:
        kb = k0 // kt
        if w.ndim == 4:
            in_specs.append(pl.BlockSpec((None, None, kt, bn), lambda i, j, kb=kb: (layer_w, j, kb, 0)))
        else:
            in_specs.append(pl.BlockSpec((None, kt, bn), lambda i, j, kb=kb: (layer_w, kb, j)))
        args.append(w)
    if bias is not None:
        in_specs.append(_row_vec_spec())
        args.append(bias.reshape(1, D))
    in_specs += [pl.BlockSpec((bm, D), lambda i, j: (i, 0)),
                 _row_vec_spec(), _mod_spec(gate_idx[0], gate_idx[1], bm)]
    args += [x, g_y.reshape(1, D), mod]
    if split_out:
        out_specs = [pl.BlockSpec((bm, D), lambda i, j: (jnp.minimum(i, p_tiles - 1), 0)),
                     pl.BlockSpec((bm, D), lambda i, j: (jnp.clip(i - p_tiles, 0, s_tiles - 1), 0))]
        out_shape = [jax.ShapeDtypeStruct((P_ROWS, D), F32), jax.ShapeDtypeStruct((S_ROWS, D), F32)]
    else:
        out_specs = [pl.BlockSpec((bm, D), lambda i, j: (i, 0))]
        out_shape = [jax.ShapeDtypeStruct((ROWS, D), F32)]
    if emit_h:
        gain, (layer, shift_chunk, scale_chunk) = next_norm
        in_specs += [_row_vec_spec(), _mod_spec(layer, shift_chunk, bm), _mod_spec(layer, scale_chunk, bm)]
        args += [gain.reshape(1, D), mod, mod]
        out_specs.append(pl.BlockSpec((bm, D), lambda i, j: (i, 0)))
        out_shape.append(jax.ShapeDtypeStruct((ROWS, D), BF16))
    kern = functools.partial(_out_kernel, terms=tuple(len(t[0]) == 2 for t in terms), nj=nj, bn=bn,
                             p_tiles=p_tiles, has_bias=bias is not None, emit_h=emit_h, split_out=split_out)
    res = pl.pallas_call(
        kern,
        grid=(ROWS // bm, nj),
        in_specs=in_specs,
        out_specs=out_specs,
        out_shape=out_shape,
        scratch_shapes=[pltpu.VMEM((nj, bm, bn), F32), pltpu.VMEM((2, 1, D), F32)],
        name="out_proj_k%d" % sum(t[2] for t in terms),
        compiler_params=_params(("arbitrary", "arbitrary")),
    )(*args)
    if split_out:
        return (res[0], res[1]), None
    return (res[0], res[1]) if emit_h else (res[0], None)


def kernel(x_prompt, x_sample, state_gla, c, c_ctx, norm_g, w_mod, b_mod, w_in_a, w_gate_up, b_gate, gla_norm_g, sg_norm_g, sg_norm_b, w_spatial, b_spatial, w_out_a, w_pw1, b_pw1, w_dw, b_dw, conv_norm_g, conv_norm_b, w_pw2, b_pw2, w_ffn_in, w_ffn_out):
    cv = jnp.concatenate([c_ctx[None, :], c, jnp.zeros((MOD_ROWS - 1 - DEC_BATCH, D), F32)], axis=0)
    mod = _modulation(cv, w_mod, b_mod).reshape(DEPTH * MOD_ROWS * N_MOD, 1, D)
    tri = _chunk_triangles()
    n_col = D // FFN_OUT_BN
    w_out_bf = w_out_a.astype(BF16)
    w_pw2_bf = w_pw2.astype(BF16)
    w_ffn_out_bf = w_ffn_out.astype(BF16).reshape(DEPTH, D_FF, n_col, FFN_OUT_BN).transpose(0, 2, 1, 3)

    x, h = _first_norm(x_prompt.reshape(P_ROWS, D), x_sample.reshape(S_ROWS, D),
                       norm_g[0, 0].reshape(1, D), mod)
    new_states = []
    for layer in range(DEPTH):
        g = norm_g[layer]
        i = layer // 2
        mid_norm = (g[2], (layer, 3, 4))
        if layer % 2 == 0:
            n_z1 = 2 * QK_W + 2 * V_W + 2 * RANK
            w_sg = w_in_a[i, :, n_z1:].astype(BF16)
            z1 = _matmul(h, w_in_a, i, Z1_W, 640, F32)
            sg_out = _spatial_gating(h, w_sg, sg_norm_g[i], sg_norm_b[i], w_spatial[i], b_spatial[i])

            wf = w_gate_up[i, 0].reshape(RANK, HEADS, DK).transpose(1, 0, 2)
            wb = w_gate_up[i, 1].reshape(RANK, HEADS, DK).transpose(1, 0, 2)
            wg = jnp.zeros((HEADS, DK, 2 * DK), F32)
            wg = wg.at[:, 0:RANK, 0:DK].set(wf).at[:, RANK:2 * RANK, DK:2 * DK].set(wb).astype(BF16)
            bg = jnp.concatenate([b_gate[i, 0].reshape(HEADS, 1, DK), b_gate[i, 1].reshape(HEADS, 1, DK)],
                                 axis=-1)
            o_p, s_new = _gla(z1, wg, bg, gla_norm_g[i], tri, SEQ, 0, BATCH, emit_state=True)
            (o_s,) = _gla(z1, wg, bg, gla_norm_g[i], tri, DEC_SEQ, P_ROWS // DEC_SEQ, DEC_BATCH,
                          s0=state_gla, layer_idx=i)
            new_states.append(s_new)

            x, h = _out_proj([((o_p, o_s), 0, V_W), ((sg_out,), V_W, SG_W)], w_out_bf, i,
                             x, mod, g[1], (layer, 2), mid_norm, bn=D)
        else:
            z = _pw1_glu(h, w_pw1, b_pw1[i], i)
            ln = (conv_norm_g[i], conv_norm_b[i])
            conv_p = _conv_rows(z, w_dw[i], b_dw[i], *ln, 0, P_ROWS, SEQ)
            if i % 2 == 1:
                conv_s = _ln_silu(_conv_cols(z, w_dw[i], b_dw[i], P_ROWS, S_ROWS, 128), *ln)
            else:
                conv_s = _conv_rows(z, w_dw[i], b_dw[i], *ln, P_ROWS, S_ROWS, GRID_W)
            x, h = _out_proj([((conv_p, conv_s), 0, D)], w_pw2_bf, i,
                             x, mod, g[1], (layer, 2), mid_norm, bn=D, bias=b_pw2[i])

        act = _ffn_in(h, w_ffn_in, layer)
        final = layer == DEPTH - 1
        next_norm = None if final else (norm_g[layer + 1, 0], (layer + 1, 0, 1))
        x, h = _out_proj([((act,), 0, D_FF)], w_ffn_out_bf, layer,
                         x, mod, g[3], (layer, 5), next_norm, bn=FFN_OUT_BN, split_out=final)

    y_prompt = x[0].reshape(BATCH, SEQ, D)
    y_sample = x[1].reshape(DEC_BATCH, DEC_SEQ, D)
    state_new = jnp.stack(new_states, axis=1).astype(x_prompt.dtype)
    return (y_prompt, y_sample, state_new)
```

```python
import functools

import jax
import jax.numpy as jnp
from jax import lax
from jax.experimental import pallas as pl
from jax.experimental.pallas import tpu as pltpu

F32 = jnp.float32
BF16 = jnp.bfloat16

D = 2048
BATCH = 16
SEQ = 256
DEPTH = 4
DEC_BATCH = 4
DEC_SEQ = 2048
GRID_W = 64
HEADS = 4
DK = 128
DV = 256
RANK = 16
TAU = 16.0
CHUNK = 64
SG_GROUPS = 4
SG_CH = 256
SG_CHUNK = 128
QK_W = HEADS * DK
V_W = HEADS * DV
SG_W = SG_GROUPS * SG_CH
CONV_K = 31
HALF_K = CONV_K // 2
D_FF = 5632
N_MOD = 6
EPS = 1e-6

P_ROWS = BATCH * SEQ
S_ROWS = DEC_BATCH * DEC_SEQ
ROWS = P_ROWS + S_ROWS
MOD_ROWS = 8
Z1_W = 2 * QK_W + 2 * V_W + 128

LANES = 128
SUBLANES = 8
BM_IN = 1024
BM_OUT = 512
FFN_OUT_BN = 512
EPI_ROWS = 64
LAG_ROWS = 16
GLA_ROWS = 4 * CHUNK
CONV_ROWS = 256
CONV_HALO = 16
VMEM_LIMIT = 56 * 1024 * 1024


def _params(sem):
    return pltpu.CompilerParams(dimension_semantics=sem, vmem_limit_bytes=VMEM_LIMIT)


def _dot(a, b):
    return jnp.dot(a, b, preferred_element_type=F32)


def _dot_nt(a, b):
    return lax.dot_general(a, b, (((1,), (1,)), ((), ())), preferred_element_type=F32)


def _dot_tn(a, b):
    return lax.dot_general(a, b, (((0,), (0,)), ((), ())), preferred_element_type=F32)


def _silu(x):
    return x / (1.0 + jnp.exp(-x))


def _rms(x, g):
    return x * lax.rsqrt(jnp.mean(x * x, axis=-1, keepdims=True) + EPS) * g


def _layer_norm(x, g, b):
    mu = jnp.mean(x, axis=-1, keepdims=True)
    xc = x - mu
    var = jnp.mean(xc * xc, axis=-1, keepdims=True)
    return xc * lax.rsqrt(var + EPS) * g + b


def _mod_row(i, bm):
    row0 = i * bm
    return jnp.where(row0 < P_ROWS, 0, 1 + (row0 - P_ROWS) // DEC_SEQ)


def _mod_spec(layer, chunk, bm, lag=0):
    def index(i, *_):
        tile = jnp.maximum(i - lag, 0)
        return ((layer * MOD_ROWS + _mod_row(tile, bm)) * N_MOD + chunk, 0, 0)
    return pl.BlockSpec((1, 1, D), index)


def _row_vec_spec():
    return pl.BlockSpec((1, D), lambda i, *_: (0, 0))


def _mod_kernel(cv_ref, w_ref, b_ref, o_ref):
    s = _silu(cv_ref[...]).astype(BF16)
    o_ref[0] = _dot(s, w_ref[0].astype(BF16)) + b_ref[0]


def _modulation(cv, w_mod, b_mod):
    bn = 1024
    n = N_MOD * D
    return pl.pallas_call(
        _mod_kernel,
        grid=(DEPTH, n // bn),
        in_specs=[
            pl.BlockSpec((MOD_ROWS, D), lambda l, j: (0, 0)),
            pl.BlockSpec((1, D, bn), lambda l, j: (l, 0, j)),
            pl.BlockSpec((1, 1, bn), lambda l, j: (l, 0, j)),
        ],
        out_specs=pl.BlockSpec((1, MOD_ROWS, bn), lambda l, j: (l, 0, j)),
        out_shape=jax.ShapeDtypeStruct((DEPTH, MOD_ROWS, n), F32),
        name="modulation",
        compiler_params=_params(("arbitrary", "arbitrary")),
    )(cv, w_mod, b_mod.reshape(DEPTH, 1, n))


def _norm_kernel(xp_ref, xs_ref, g_ref, sh_ref, sc_ref, x_ref, h_ref, *, p_tiles):
    def emit(src_ref):
        x = src_ref[...]
        x_ref[...] = x
        h = _rms(x, g_ref[...]) * (1.0 + sc_ref[0]) + sh_ref[0]
        h_ref[...] = h.astype(BF16)

    i = pl.program_id(0)
    pl.when(i < p_tiles)(lambda: emit(xp_ref))
    pl.when(i >= p_tiles)(lambda: emit(xs_ref))


def _first_norm(x_prompt, x_sample, g, mod):
    bm = 256
    p_tiles = P_ROWS // bm
    s_tiles = S_ROWS // bm
    return pl.pallas_call(
        functools.partial(_norm_kernel, p_tiles=p_tiles),
        grid=(ROWS // bm,),
        in_specs=[
            pl.BlockSpec((bm, D), lambda i: (jnp.minimum(i, p_tiles - 1), 0)),
            pl.BlockSpec((bm, D), lambda i: (jnp.clip(i - p_tiles, 0, s_tiles - 1), 0)),
            _row_vec_spec(),
            _mod_spec(0, 0, bm),
            _mod_spec(0, 1, bm),
        ],
        out_specs=[pl.BlockSpec((bm, D), lambda i: (i, 0)), pl.BlockSpec((bm, D), lambda i: (i, 0))],
        out_shape=[jax.ShapeDtypeStruct((ROWS, D), F32), jax.ShapeDtypeStruct((ROWS, D), BF16)],
        name="first_norm",
        compiler_params=_params(("arbitrary",)),
    )(x_prompt, x_sample, g, mod, mod)


def _cast_weights_once(pairs):
    @pl.when(pl.program_id(1) == 0)
    def _():
        for w_ref, w_scr in pairs:
            w_scr[...] = w_ref[...].astype(BF16)


def _w_spec(layer, bn, block0=0):
    return pl.BlockSpec((None, D, bn), lambda j, i: (layer, 0, block0 + j))


def _mm_kernel(h_ref, w_ref, o_ref, w_scr):
    _cast_weights_once([(w_ref, w_scr)])
    o_ref[...] = _dot(h_ref[...], w_scr[...]).astype(o_ref.dtype)


def _matmul(h, w, layer, n, bn, out_dtype):
    bm = BM_IN
    return pl.pallas_call(
        _mm_kernel,
        grid=(n // bn, ROWS // bm),
        in_specs=[pl.BlockSpec((bm, D), lambda j, i: (i, 0)), _w_spec(layer, bn)],
        out_specs=pl.BlockSpec((bm, bn), lambda j, i: (i, j)),
        out_shape=jax.ShapeDtypeStruct((ROWS, n), out_dtype),
        scratch_shapes=[pltpu.VMEM((D, bn), BF16)],
        name="in_proj",
        compiler_params=_params(("arbitrary", "arbitrary")),
    )(h, w)


def _swiglu_kernel(h_ref, wa_ref, wb_ref, o_ref, wa_scr, wb_scr):
    _cast_weights_once([(wa_ref, wa_scr), (wb_ref, wb_scr)])
    h = h_ref[...]
    a = _dot(h, wa_scr[...])
    b = _dot(h, wb_scr[...])
    o_ref[...] = (_silu(a) * b).astype(o_ref.dtype)


def _ffn_in(h, w, layer):
    bm, bn = BM_IN, 512
    nb = D_FF // bn
    return pl.pallas_call(
        _swiglu_kernel,
        grid=(nb, ROWS // bm),
        in_specs=[pl.BlockSpec((bm, D), lambda j, i: (i, 0)), _w_spec(layer, bn), _w_spec(layer, bn, nb)],
        out_specs=pl.BlockSpec((bm, bn), lambda j, i: (i, j)),
        out_shape=jax.ShapeDtypeStruct((ROWS, D_FF), BF16),
        scratch_shapes=[pltpu.VMEM((D, bn), BF16), pltpu.VMEM((D, bn), BF16)],
        name="ffn_in",
        compiler_params=_params(("arbitrary", "arbitrary")),
    )(h, w, w)


def _glu_kernel(h_ref, wa_ref, wb_ref, ba_ref, bb_ref, o_ref, wa_scr, wb_scr):
    _cast_weights_once([(wa_ref, wa_scr), (wb_ref, wb_scr)])
    h = h_ref[...]
    a = _dot(h, wa_scr[...]) + ba_ref[...]
    gt = _dot(h, wb_scr[...]) + bb_ref[...]
    o_ref[...] = a * (1.0 / (1.0 + jnp.exp(-gt)))


def _pw1_glu(h, w, b, layer):
    bm, bn = BM_IN, 512
    nb = D // bn
    b2 = b.reshape(1, 2 * D)
    return pl.pallas_call(
        _glu_kernel,
        grid=(nb, ROWS // bm),
        in_specs=[
            pl.BlockSpec((bm, D), lambda j, i: (i, 0)),
            _w_spec(layer, bn),
            _w_spec(layer, bn, nb),
            pl.BlockSpec((1, bn), lambda j, i: (0, j)),
            pl.BlockSpec((1, bn), lambda j, i: (0, j + nb)),
        ],
        out_specs=pl.BlockSpec((bm, bn), lambda j, i: (i, j)),
        out_shape=jax.ShapeDtypeStruct((ROWS, D), F32),
        scratch_shapes=[pltpu.VMEM((D, bn), BF16), pltpu.VMEM((D, bn), BF16)],
        name="pw1_glu",
        compiler_params=_params(("arbitrary", "arbitrary")),
    )(h, w, w, b2, b2)


def _gelu_tanh(x):
    return 0.5 * x * (1.0 + jnp.tanh(0.7978845608028654 * (x + 0.044715 * (x * x * x))))


def _sg_kernel(h_ref, wu_ref, wv_ref, g_ref, b_ref, wsp_ref, bsp_ref, o_ref):
    h = h_ref[...]
    u = _gelu_tanh(_dot(h, wu_ref[...]))
    vv = _gelu_tanh(_dot(h, wv_ref[...]))
    vv = _layer_norm(vv, g_ref[0], b_ref[0]).astype(BF16)
    wsp = wsp_ref[0]
    bsp = bsp_ref[0]
    for c in range(h.shape[0] // SG_CHUNK):
        rows = slice(c * SG_CHUNK, (c + 1) * SG_CHUNK)
        mixed = _dot(wsp, vv[rows]) + bsp
        o_ref[rows, :] = (u[rows] * mixed).astype(o_ref.dtype)


def _spatial_gating(h, w_sg, sg_g, sg_b, w_sp, b_sp):
    bm = BM_IN
    return pl.pallas_call(
        _sg_kernel,
        grid=(ROWS // bm, SG_GROUPS),
        in_specs=[
            pl.BlockSpec((bm, D), lambda i, g: (i, 0)),
            pl.BlockSpec((D, SG_CH), lambda i, g: (0, g)),
            pl.BlockSpec((D, SG_CH), lambda i, g: (0, g + SG_GROUPS)),
            pl.BlockSpec((1, 1, SG_CH), lambda i, g: (g, 0, 0)),
            pl.BlockSpec((1, 1, SG_CH), lambda i, g: (g, 0, 0)),
            pl.BlockSpec((1, SG_CHUNK, SG_CHUNK), lambda i, g: (g, 0, 0)),
            pl.BlockSpec((1, SG_CHUNK, 1), lambda i, g: (g, 0, 0)),
        ],
        out_specs=pl.BlockSpec((bm, SG_CH), lambda i, g: (i, g)),
        out_shape=jax.ShapeDtypeStruct((ROWS, SG_W), BF16),
        name="spatial_gating",
        compiler_params=_params(("arbitrary", "arbitrary")),
    )(h, w_sg, w_sg, sg_g.reshape(SG_GROUPS, 1, SG_CH), sg_b.reshape(SG_GROUPS, 1, SG_CH),
      w_sp.astype(BF16), b_sp.reshape(SG_GROUPS, SG_CHUNK, 1))


def _chunk_triangles():
    i = jnp.arange(GLA_ROWS)[:, None]
    j = jnp.arange(GLA_ROWS)[None, :]
    same = (i // CHUNK) == (j // CHUNK)
    return jnp.stack([same & (j <= i), same & (j >= i)]).astype(BF16)


def _gla_kernel(*refs, seq, has_init, emit_state):
    q_ref, k_ref, v_ref, r_ref, g_ref, wg_ref, bg_ref, gg_ref, tri_ref = refs[:9]
    pos = 9
    s0_ref = None
    if has_init:
        s0_ref = refs[pos]
        pos += 1
    o_ref = refs[pos]
    pos += 1
    sfin_ref = None
    if emit_state:
        sfin_ref = refs[pos]
        pos += 1
    b_scr, of_scr, ob_scr = refs[pos:pos + 3]

    n_steps = seq // GLA_ROWS
    n_sub = GLA_ROWS // CHUNK
    scale = DK ** -0.5

    def decay_step(s, carry):
        rows = pl.ds(pl.multiple_of(s * GLA_ROWS, GLA_ROWS), GLA_ROWS)
        logits = _dot(g_ref[rows, :].astype(BF16), wg_ref[0]) + bg_ref[0]
        la = (jnp.minimum(logits, 0.0) - jnp.log1p(jnp.exp(-jnp.abs(logits)))) / TAU
        hi = la.astype(BF16)
        lo = (la - hi.astype(F32)).astype(BF16)
        tri = tri_ref[0]
        pre = _dot(tri, hi) + _dot(tri, lo)
        b_scr[rows, 0:DK] = pre[:, 0:DK]
        for c in range(n_sub):
            cs = slice(c * CHUNK, (c + 1) * CHUNK)
            total = pre[(c + 1) * CHUNK - 1:(c + 1) * CHUNK, DK:]
            b_scr[pl.ds(s * GLA_ROWS + c * CHUNK, CHUNK), DK:2 * DK] = total - pre[cs, DK:] + la[cs, DK:]
        return carry

    lax.fori_loop(0, n_steps, decay_step, 0)

    def rows_step(r0, direction, st, out_scr):
        rows = pl.ds(r0, GLA_ROWS)
        b = b_scr[rows, direction * DK:(direction + 1) * DK]
        last = [(c + 1) * CHUNK - 1 if direction == 0 else c * CHUNK for c in range(n_sub)]
        rem = jnp.concatenate(
            [b[last[c]:last[c] + 1, :] - b[c * CHUNK:(c + 1) * CHUNK, :] for c in range(n_sub)], axis=0)
        q = q_ref[rows, :] * scale
        k = k_ref[rows, :]
        v = v_ref[rows, :].astype(BF16)
        q_in = (q * jnp.exp(b)).astype(BF16)
        k_in = (k * jnp.exp(-b)).astype(BF16)
        k_out = (k * jnp.exp(rem)).astype(BF16)
        mask = tri_ref[direction] > 0
        att = jnp.where(mask, _dot_nt(q_in, k_in), 0.0).astype(BF16)
        o_intra = _dot(att, v)
        order = range(n_sub) if direction == 0 else range(n_sub - 1, -1, -1)
        for c in order:
            cs = slice(c * CHUNK, (c + 1) * CHUNK)
            out_scr[pl.ds(r0 + c * CHUNK, CHUNK), :] = o_intra[cs] + _dot_nt(q_in[cs], st.astype(BF16))
            st = st * jnp.exp(b[last[c]:last[c] + 1, :]) + _dot_tn(v[cs], k_out[cs])
        return st

    def body(s, carry):
        st_f, st_b = carry
        rf = pl.multiple_of(s * GLA_ROWS, GLA_ROWS)
        rb = pl.multiple_of((n_steps - 1 - s) * GLA_ROWS, GLA_ROWS)
        st_f = rows_step(rf, 0, st_f, of_scr)
        st_b = rows_step(rb, 1, st_b, ob_scr)
        return st_f, st_b

    if has_init:
        init = (s0_ref[0].T, s0_ref[1].T)
    else:
        init = (jnp.zeros((DV, DK), F32), jnp.zeros((DV, DK), F32))
    st_f, st_b = lax.fori_loop(0, n_steps, body, init)

    if emit_state:
        sfin_ref[0] = st_f.T
        sfin_ref[1] = st_b.T

    gg = gg_ref[0]

    def finish(i, carry):
        rows = pl.ds(pl.multiple_of(i * GLA_ROWS, GLA_ROWS), GLA_ROWS)
        o = _rms(of_scr[rows, :] + ob_scr[rows, :], gg)
        o_ref[rows, :] = (o * _silu(r_ref[rows, :])).astype(o_ref.dtype)
        return carry

    lax.fori_loop(0, n_steps, finish, 0)


def _gla(z1, wg, bg, gla_g, tri, seq, row_block0, n_seq, s0=None, layer_idx=0, emit_state=False):
    qk_blocks = QK_W // DK
    v_block0 = 2 * QK_W // DV
    r_block0 = v_block0 + HEADS
    g_block = (2 * QK_W + 2 * V_W) // DK
    rb = row_block0
    in_specs = [
        pl.BlockSpec((seq, DK), lambda b, h: (rb + b, h)),
        pl.BlockSpec((seq, DK), lambda b, h: (rb + b, qk_blocks + h)),
        pl.BlockSpec((seq, DV), lambda b, h: (rb + b, v_block0 + h)),
        pl.BlockSpec((seq, DV), lambda b, h: (rb + b, r_block0 + h)),
        pl.BlockSpec((seq, DK), lambda b, h: (rb + b, g_block)),
        pl.BlockSpec((1, DK, 2 * DK), lambda b, h: (h, 0, 0)),
        pl.BlockSpec((1, 1, 2 * DK), lambda b, h: (h, 0, 0)),
        pl.BlockSpec((1, 1, DV), lambda b, h: (h, 0, 0)),
        pl.BlockSpec((2, GLA_ROWS, GLA_ROWS), lambda b, h: (0, 0, 0)),
    ]
    args = [z1, z1, z1, z1, z1, wg, bg, gla_g.reshape(HEADS, 1, DV), tri]
    if s0 is not None:
        in_specs.append(pl.BlockSpec((None, None, 2, None, DK, DV),
                                     lambda b, h: (b, layer_idx, 0, h, 0, 0)))
        args.append(s0)
    out_specs = [pl.BlockSpec((seq, DV), lambda b, h: (b, h))]
    out_shape = [jax.ShapeDtypeStruct((n_seq * seq, V_W), BF16)]
    if emit_state:
        out_specs.append(pl.BlockSpec((None, 2, None, DK, DV), lambda b, h: (b, 0, h, 0, 0)))
        out_shape.append(jax.ShapeDtypeStruct((n_seq, 2, HEADS, DK, DV), F32))
    kern = functools.partial(_gla_kernel, seq=seq, has_init=s0 is not None, emit_state=emit_state)
    return pl.pallas_call(
        kern,
        grid=(n_seq, HEADS),
        in_specs=in_specs,
        out_specs=out_specs,
        out_shape=out_shape,
        scratch_shapes=[
            pltpu.VMEM((seq, 2 * DK), F32),
            pltpu.VMEM((seq, DV), F32),
            pltpu.VMEM((seq, DV), F32),
        ],
        name="gla_seq%d" % seq,
        compiler_params=_params(("arbitrary", "arbitrary")),
    )(*args)


def _ln_silu_rows(src_ref, g_ref, b_ref, o_ref):
    def step(r, carry):
        rows = pl.ds(pl.multiple_of(r * EPI_ROWS, EPI_ROWS), EPI_ROWS)
        y = _layer_norm(src_ref[rows, :], g_ref[...], b_ref[...])
        o_ref[rows, :] = _silu(y).astype(o_ref.dtype)
        return carry

    lax.fori_loop(0, o_ref.shape[0] // EPI_ROWS, step, 0, unroll=2)


def _conv_rows_kernel(z_ref, w_ref, b_ref, lg_ref, lb_ref, o_ref, pad_scr, conv_scr, *, seg):
    rows = z_ref.shape[0]
    n_seg = rows // seg
    plen = seg + 2 * CONV_HALO
    zeros = jnp.zeros((CONV_HALO, LANES), F32)
    first = CONV_HALO - HALF_K
    for ct in range(D // LANES):
        cols = slice(ct * LANES, (ct + 1) * LANES)
        for s in range(n_seg):
            base = s * plen
            pad_scr[ct, base:base + CONV_HALO, :] = zeros
            pad_scr[ct, base + CONV_HALO:base + CONV_HALO + seg, :] = z_ref[s * seg:(s + 1) * seg, cols]
            pad_scr[ct, base + CONV_HALO + seg:base + plen, :] = zeros
        acc = [jnp.zeros((seg, LANES), F32) + b_ref[:, cols] for _ in range(n_seg)]
        for t in range(CONV_K):
            tap = w_ref[t:t + 1, cols]
            for s in range(n_seg):
                start = s * plen + first + t
                acc[s] = acc[s] + pad_scr[ct, start:start + seg, :] * tap
        for s in range(n_seg):
            conv_scr[s * seg:(s + 1) * seg, cols] = acc[s]

    _ln_silu_rows(conv_scr, lg_ref, lb_ref, o_ref)


def _conv_rows(z, w, b, ln_g, ln_b, row0, n_rows, seg):
    rows = CONV_ROWS
    rb0 = row0 // rows
    plen = seg + 2 * CONV_HALO
    return pl.pallas_call(
        functools.partial(_conv_rows_kernel, seg=seg),
        grid=(n_rows // rows,),
        in_specs=[
            pl.BlockSpec((rows, D), lambda i: (rb0 + i, 0)),
            pl.BlockSpec((CONV_K, D), lambda i: (0, 0)),
            _row_vec_spec(),
            _row_vec_spec(),
            _row_vec_spec(),
        ],
        out_specs=pl.BlockSpec((rows, D), lambda i: (i, 0)),
        out_shape=jax.ShapeDtypeStruct((n_rows, D), BF16),
        scratch_shapes=[
            pltpu.VMEM((D // LANES, (rows // seg) * plen, LANES), F32),
            pltpu.VMEM((rows, D), F32),
        ],
        name="conv_rows_seg%d" % seg,
        compiler_params=_params(("arbitrary",)),
    )(z, w, b.reshape(1, D), ln_g.reshape(1, D), ln_b.reshape(1, D))


def _conv_cols_kernel(z_ref, w_ref, b_ref, o_ref, pad_scr):
    rows, ct = z_ref.shape
    halo = HALF_K * GRID_W
    zeros = jnp.zeros((halo, ct), F32)
    pad_scr[0:halo, :] = zeros
    pad_scr[halo:halo + rows, :] = z_ref[...]
    pad_scr[halo + rows:2 * halo + rows, :] = zeros
    blk = 256

    def body(i, carry):
        r0 = pl.multiple_of(i * blk, blk)
        acc = jnp.zeros((blk, ct), F32) + b_ref[...]
        for t in range(CONV_K):
            acc = acc + pad_scr[pl.ds(r0 + t * GRID_W, blk), :] * w_ref[t:t + 1, :]
        o_ref[pl.ds(r0, blk), :] = acc
        return carry

    lax.fori_loop(0, rows // blk, body, 0)


def _conv_cols(z, w, b, row0, n_rows, ct):
    rows = DEC_SEQ
    rb0 = row0 // rows
    return pl.pallas_call(
        _conv_cols_kernel,
        grid=(n_rows // rows, D // ct),
        in_specs=[
            pl.BlockSpec((rows, ct), lambda i, j: (rb0 + i, j)),
            pl.BlockSpec((CONV_K, ct), lambda i, j: (0, j)),
            pl.BlockSpec((1, ct), lambda i, j: (0, j)),
        ],
        out_specs=pl.BlockSpec((rows, ct), lambda i, j: (i, j)),
        out_shape=jax.ShapeDtypeStruct((n_rows, D), F32),
        scratch_shapes=[pltpu.VMEM((rows + 2 * HALF_K * GRID_W, ct), F32)],
        name="conv_cols",
        compiler_params=_params(("arbitrary", "arbitrary")),
    )(z, w, b.reshape(1, D))


def _ln_silu_kernel(y_ref, g_ref, b_ref, o_ref):
    _ln_silu_rows(y_ref, g_ref, b_ref, o_ref)


def _ln_silu(y, ln_g, ln_b):
    bm = 512
    n_rows = y.shape[0]
    return pl.pallas_call(
        _ln_silu_kernel,
        grid=(n_rows // bm,),
        in_specs=[pl.BlockSpec((bm, D), lambda i: (i, 0)), _row_vec_spec(), _row_vec_spec()],
        out_specs=pl.BlockSpec((bm, D), lambda i: (i, 0)),
        out_shape=jax.ShapeDtypeStruct((n_rows, D), BF16),
        name="ln_silu",
        compiler_params=_params(("arbitrary",)),
    )(y, ln_g.reshape(1, D), ln_b.reshape(1, D))


def _out_kernel(*refs, terms, nj, bn, p_tiles, has_bias, emit_h, split_out):
    pos = 0
    lhs = []
    for split in terms:
        n = 2 if split else 1
        lhs.append(refs[pos:pos + n])
        pos += n
    w_refs = refs[pos:pos + len(terms)]
    pos += len(terms)
    if has_bias:
        bias_ref = refs[pos]
        pos += 1
    x_ref, ga_ref, gate_ref = refs[pos:pos + 3]
    pos += 3
    if emit_h:
        gb_ref, sh_ref, sc_ref = refs[pos:pos + 3]
        pos += 3
    n_out = 2 if split_out else 1
    xo_refs = refs[pos:pos + n_out]
    pos += n_out
    if emit_h:
        h_ref = refs[pos]
        pos += 1
    y_scr, vec_scr = refs[pos:pos + 2]

    i = pl.program_id(0)
    j = pl.program_id(1)

    def column_tile(latent):
        y = None
        for l, w in zip(lhs, w_refs):
            a = l[1 if (latent and len(l) == 2) else 0][...]
            d = _dot(a, w[...])
            y = d if y is None else y + d
        y_scr[j] = y

    if any(terms):
        pl.when(i < p_tiles)(lambda: column_tile(False))
        pl.when(i >= p_tiles)(lambda: column_tile(True))
    else:
        column_tile(False)

    col_slices = [slice(jj * bn, (jj + 1) * bn) for jj in range(nj)]

    def finish(xo_ref):
        vec_scr[0] = gate_ref[0] * ga_ref[...]
        if emit_h:
            vec_scr[1] = gb_ref[...] * (1.0 + sc_ref[0])

        def rows_step(r, carry):
            rows = pl.ds(pl.multiple_of(r * EPI_ROWS, EPI_ROWS), EPI_ROWS)
            ys = []
            ssq = None
            for jj, cols in enumerate(col_slices):
                y = y_scr[jj, rows, :]
                if has_bias:
                    y = y + bias_ref[:, cols]
                ys.append(y)
                part = jnp.sum(y * y, axis=-1, keepdims=True)
                ssq = part if ssq is None else ssq + part
            inv = lax.rsqrt(ssq * (1.0 / D) + EPS)
            xs = []
            ssq = None
            for jj, cols in enumerate(col_slices):
                x_new = x_ref[rows, cols] + ys[jj] * inv * vec_scr[0, :, cols]
                xo_ref[rows, cols] = x_new
                if emit_h:
                    xs.append(x_new)
                    part = jnp.sum(x_new * x_new, axis=-1, keepdims=True)
                    ssq = part if ssq is None else ssq + part
            if emit_h:
                inv = lax.rsqrt(ssq * (1.0 / D) + EPS)
                for jj, cols in enumerate(col_slices):
                    h = xs[jj] * inv * vec_scr[1, :, cols] + sh_ref[0, :, cols]
                    h_ref[rows, cols] = h.astype(BF16)
            return carry

        lax.fori_loop(0, xo_ref.shape[0] // EPI_ROWS, rows_step, 0, unroll=2)

    last = j == nj - 1
    if split_out:
        pl.when(last & (i < p_tiles))(lambda: finish(xo_refs[0]))
        pl.when(last & (i >= p_tiles))(lambda: finish(xo_refs[1]))
    else:
        pl.when(last)(lambda: finish(xo_refs[0]))


def _out_proj(terms, w, layer_w, x, mod, g_y, gate_idx, next_norm, *, bn, bias=None, split_out=False):
    bm = BM_OUT
    nj = D // bn
    p_tiles = P_ROWS // bm
    s_tiles = S_ROWS // bm
    emit_h = next_norm is not None
    in_specs, args = [], []
    for arrays, _, kt in terms:
        if len(arrays) == 2:
            in_specs += [
                pl.BlockSpec((bm, kt), lambda i, j: (jnp.minimum(i, p_tiles - 1), 0)),
                pl.BlockSpec((bm, kt), lambda i, j: (jnp.clip(i - p_tiles, 0, s_tiles - 1), 0)),
            ]
        else:
            in_specs.append(pl.BlockSpec((bm, kt), lambda i, j: (i, 0)))
        args += list(arrays)
    for _, k0, kt in terms:
        kb = k0 // kt
        if w.ndim == 4:
            in_specs.append(pl.BlockSpec((None, None, kt, bn), lambda i, j, kb=kb: (layer_w, j, kb, 0)))
        else:
            in_specs.append(pl.BlockSpec((None, kt, bn), lambda i, j, kb=kb: (layer_w, kb, j)))
        args.append(w)
    if bias is not None:
        in_specs.append(_row_vec_spec())
        args.append(bias.reshape(1, D))
    in_specs += [pl.BlockSpec((bm, D), lambda i, j: (i, 0)),
                 _row_vec_spec(), _mod_spec(gate_idx[0], gate_idx[1], bm)]
    args += [x, g_y.reshape(1, D), mod]
    if split_out:
        out_specs = [pl.BlockSpec((bm, D), lambda i, j: (jnp.minimum(i, p_tiles - 1), 0)),
                     pl.BlockSpec((bm, D), lambda i, j: (jnp.clip(i - p_tiles, 0, s_tiles - 1), 0))]
        out_shape = [jax.ShapeDtypeStruct((P_ROWS, D), F32), jax.ShapeDtypeStruct((S_ROWS, D), F32)]
    else:
        out_specs = [pl.BlockSpec((bm, D), lambda i, j: (i, 0))]
        out_shape = [jax.ShapeDtypeStruct((ROWS, D), F32)]
    if emit_h:
        gain, (layer, shift_chunk, scale_chunk) = next_norm
        in_specs += [_row_vec_spec(), _mod_spec(layer, shift_chunk, bm), _mod_spec(layer, scale_chunk, bm)]
        args += [gain.reshape(1, D), mod, mod]
        out_specs.append(pl.BlockSpec((bm, D), lambda i, j: (i, 0)))
        out_shape.append(jax.ShapeDtypeStruct((ROWS, D), BF16))
    kern = functools.partial(_out_kernel, terms=tuple(len(t[0]) == 2 for t in terms), nj=nj, bn=bn,
                             p_tiles=p_tiles, has_bias=bias is not None, emit_h=emit_h, split_out=split_out)
    res = pl.pallas_call(
        kern,
        grid=(ROWS // bm, nj),
        in_specs=in_specs,
        out_specs=out_specs,
        out_shape=out_shape,
        scratch_shapes=[pltpu.VMEM((nj, bm, bn), F32), pltpu.VMEM((2, 1, D), F32)],
        name="out_proj_k%d" % sum(t[2] for t in terms),
        compiler_params=_params(("arbitrary", "arbitrary")),
    )(*args)
    if split_out:
        return (res[0], res[1]), None
    return (res[0], res[1]) if emit_h else (res[0], None)


def _out_lagged_kernel(*refs, terms, n_tiles, p_tiles, has_bias):
    pos = 0
    lhs = []
    for split in terms:
        n = 2 if split else 1
        lhs.append(refs[pos:pos + n])
        pos += n
    w_refs = refs[pos:pos + len(terms)]
    pos += len(terms)
    if has_bias:
        bias_ref = refs[pos]
        pos += 1
    x_ref, ga_ref, gate_ref, gb_ref, sh_ref, sc_ref, xo_ref, h_ref, y_even, y_odd, vec_scr = refs[pos:pos + 11]

    i = pl.program_id(0)
    bm = xo_ref.shape[0]

    @pl.when(i == 0)
    def _():
        y_odd[...] = jnp.zeros_like(y_odd)

    def step(y_write, y_read):
        vec_scr[0] = gate_ref[0] * ga_ref[...]
        vec_scr[1] = gb_ref[...] * (1.0 + sc_ref[0])
        for r in range(bm // LAG_ROWS):
            rows = slice(r * LAG_ROWS, (r + 1) * LAG_ROWS)
            yr = y_read[rows, :]
            if has_bias:
                yr = yr + bias_ref[...]
            inv = lax.rsqrt(jnp.sum(yr * yr, axis=-1, keepdims=True) * (1.0 / D) + EPS)
            x_new = x_ref[rows, :] + yr * inv * vec_scr[0]
            xo_ref[rows, :] = x_new
            inv = lax.rsqrt(jnp.sum(x_new * x_new, axis=-1, keepdims=True) * (1.0 / D) + EPS)
            h_ref[rows, :] = (x_new * inv * vec_scr[1] + sh_ref[0]).astype(BF16)

        y = None
        for l, w in zip(lhs, w_refs):
            a = jnp.where(i < p_tiles, l[0][...], l[1][...]) if len(l) == 2 else l[0][...]
            d = _dot(a, w[...])
            y = d if y is None else y + d
        y_write[...] = y

    pl.when(i % 2 == 0)(lambda: step(y_even, y_odd))
    pl.when(i % 2 == 1)(lambda: step(y_odd, y_even))


def _out_proj_lagged(terms, w, layer_w, x, mod, g_y, gate_idx, next_norm, *, bias=None):
    bm = BM_OUT
    n_tiles = ROWS // bm
    p_tiles = P_ROWS // bm
    s_tiles = S_ROWS // bm
    in_specs, args = [], []
    for arrays, _, kt in terms:
        if len(arrays) == 2:
            in_specs += [
                pl.BlockSpec((bm, kt), lambda i: (jnp.minimum(i, p_tiles - 1), 0)),
                pl.BlockSpec((bm, kt), lambda i: (jnp.clip(i - p_tiles, 0, s_tiles - 1), 0)),
            ]
        else:
            in_specs.append(pl.BlockSpec((bm, kt), lambda i: (jnp.minimum(i, n_tiles - 1), 0)))
        args += list(arrays)
    for _, k0, kt in terms:
        in_specs.append(pl.BlockSpec((None, kt, D), lambda i, kb=k0 // kt: (layer_w, kb, 0)))
        args.append(w)
    if bias is not None:
        in_specs.append(_row_vec_spec())
        args.append(bias.reshape(1, D))
    prev = lambda i: (jnp.maximum(i - 1, 0), 0)
    gain, (layer, shift_chunk, scale_chunk) = next_norm
    in_specs += [pl.BlockSpec((bm, D), prev), _row_vec_spec(), _mod_spec(gate_idx[0], gate_idx[1], bm, lag=1),
                 _row_vec_spec(), _mod_spec(layer, shift_chunk, bm, lag=1), _mod_spec(layer, scale_chunk, bm, lag=1)]
    args += [x, g_y.reshape(1, D), mod, gain.reshape(1, D), mod, mod]
    kern = functools.partial(_out_lagged_kernel, terms=tuple(len(t[0]) == 2 for t in terms),
                             n_tiles=n_tiles, p_tiles=p_tiles, has_bias=bias is not None)
    x_new, h = pl.pallas_call(
        kern,
        grid=(n_tiles + 1,),
        in_specs=in_specs,
        out_specs=[pl.BlockSpec((bm, D), prev), pl.BlockSpec((bm, D), prev)],
        out_shape=[jax.ShapeDtypeStruct((ROWS, D), F32), jax.ShapeDtypeStruct((ROWS, D), BF16)],
        scratch_shapes=[pltpu.VMEM((bm, D), F32), pltpu.VMEM((bm, D), F32), pltpu.VMEM((2, 1, D), F32)],
        name="out_proj_lagged_k%d" % sum(t[2] for t in terms),
        compiler_params=_params(("arbitrary",)),
    )(*args)
    return x_new, h


def kernel(x_prompt, x_sample, state_gla, c, c_ctx, norm_g, w_mod, b_mod, w_in_a, w_gate_up, b_gate, gla_norm_g, sg_norm_g, sg_norm_b, w_spatial, b_spatial, w_out_a, w_pw1, b_pw1, w_dw, b_dw, conv_norm_g, conv_norm_b, w_pw2, b_pw2, w_ffn_in, w_ffn_out):
    cv = jnp.concatenate([c_ctx[None, :], c, jnp.zeros((MOD_ROWS - 1 - DEC_BATCH, D), F32)], axis=0)
    mod = _modulation(cv, w_mod, b_mod).reshape(DEPTH * MOD_ROWS * N_MOD, 1, D)
    tri = _chunk_triangles()
    n_col = D // FFN_OUT_BN
    w_out_bf = w_out_a.astype(BF16)
    w_pw2_bf = w_pw2.astype(BF16)
    w_ffn_out_bf = w_ffn_out.astype(BF16).reshape(DEPTH, D_FF, n_col, FFN_OUT_BN).transpose(0, 2, 1, 3)

    x, h = _first_norm(x_prompt.reshape(P_ROWS, D), x_sample.reshape(S_ROWS, D),
                       norm_g[0, 0].reshape(1, D), mod)
    new_states = []
    for layer in range(DEPTH):
        g = norm_g[layer]
        i = layer // 2
        mid_norm = (g[2], (layer, 3, 4))
        if layer % 2 == 0:
            n_z1 = 2 * QK_W + 2 * V_W + 2 * RANK
            w_sg = w_in_a[i, :, n_z1:].astype(BF16)
            z1 = _matmul(h, w_in_a, i, Z1_W, 640, F32)
            sg_out = _spatial_gating(h, w_sg, sg_norm_g[i], sg_norm_b[i], w_spatial[i], b_spatial[i])

            wf = w_gate_up[i, 0].reshape(RANK, HEADS, DK).transpose(1, 0, 2)
            wb = w_gate_up[i, 1].reshape(RANK, HEADS, DK).transpose(1, 0, 2)
            wg = jnp.zeros((HEADS, DK, 2 * DK), F32)
            wg = wg.at[:, 0:RANK, 0:DK].set(wf).at[:, RANK:2 * RANK, DK:2 * DK].set(wb).astype(BF16)
            bg = jnp.concatenate([b_gate[i, 0].reshape(HEADS, 1, DK), b_gate[i, 1].reshape(HEADS, 1, DK)],
                                 axis=-1)
            o_p, s_new = _gla(z1, wg, bg, gla_norm_g[i], tri, SEQ, 0, BATCH, emit_state=True)
            (o_s,) = _gla(z1, wg, bg, gla_norm_g[i], tri, DEC_SEQ, P_ROWS // DEC_SEQ, DEC_BATCH,
                          s0=state_gla, layer_idx=i)
            new_states.append(s_new)

            x, h = _out_proj_lagged([((o_p, o_s), 0, V_W), ((sg_out,), V_W, SG_W)], w_out_bf, i,
                                    x, mod, g[1], (layer, 2), mid_norm)
        else:
            z = _pw1_glu(h, w_pw1, b_pw1[i], i)
            ln = (conv_norm_g[i], conv_norm_b[i])
            conv_p = _conv_rows(z, w_dw[i], b_dw[i], *ln, 0, P_ROWS, SEQ)
            if i % 2 == 1:
                conv_s = _ln_silu(_conv_cols(z, w_dw[i], b_dw[i], P_ROWS, S_ROWS, 128), *ln)
            else:
                conv_s = _conv_rows(z, w_dw[i], b_dw[i], *ln, P_ROWS, S_ROWS, GRID_W)
            x, h = _out_proj_lagged([((conv_p, conv_s), 0, D)], w_pw2_bf, i,
                                    x, mod, g[1], (layer, 2), mid_norm, bias=b_pw2[i])

        act = _ffn_in(h, w_ffn_in, layer)
        final = layer == DEPTH - 1
        next_norm = None if final else (norm_g[layer + 1, 0], (layer + 1, 0, 1))
        x, h = _out_proj([((act,), 0, D_FF)], w_ffn_out_bf, layer,
                         x, mod, g[3], (layer, 5), next_norm, bn=FFN_OUT_BN, split_out=final)

    y_prompt = x[0].reshape(BATCH, SEQ, D)
    y_sample = x[1].reshape(DEC_BATCH, DEC_SEQ, D)
    state_new = jnp.stack(new_states, axis=1).astype(x_prompt.dtype)
    return (y_prompt, y_sample, state_new)
```

```python
import functools

import jax
import jax.numpy as jnp
from jax import lax
from jax.experimental import pallas as pl
from jax.experimental.pallas import tpu as pltpu

F32 = jnp.float32
BF16 = jnp.bfloat16

D = 2048
BATCH = 16
SEQ = 256
DEPTH = 4
DEC_BATCH = 4
DEC_SEQ = 2048
GRID_W = 64
HEADS = 4
DK = 128
DV = 256
RANK = 16
TAU = 16.0
CHUNK = 64
SG_GROUPS = 4
SG_CH = 256
SG_CHUNK = 128
QK_W = HEADS * DK
V_W = HEADS * DV
SG_W = SG_GROUPS * SG_CH
CONV_K = 31
HALF_K = CONV_K // 2
D_FF = 5632
N_MOD = 6
EPS = 1e-6

P_ROWS = BATCH * SEQ
S_ROWS = DEC_BATCH * DEC_SEQ
ROWS = P_ROWS + S_ROWS
MOD_ROWS = 8
Z1_W = 2 * QK_W + 2 * V_W + 128

LANES = 128
SUBLANES = 8
BM_IN = 1024
BM_IN_WIDE = 2048
BM_OUT = 512
FFN_OUT_BN = 512
EPI_ROWS = 64
LAG_ROWS = 16
GLA_ROWS = 4 * CHUNK
CONV_ROWS = 256
CONV_HALO = 16
VMEM_LIMIT = 56 * 1024 * 1024


def _params(sem):
    return pltpu.CompilerParams(dimension_semantics=sem, vmem_limit_bytes=VMEM_LIMIT)


def _dot(a, b):
    return jnp.dot(a, b, preferred_element_type=F32)


def _dot_nt(a, b):
    return lax.dot_general(a, b, (((1,), (1,)), ((), ())), preferred_element_type=F32)


def _dot_tn(a, b):
    return lax.dot_general(a, b, (((0,), (0,)), ((), ())), preferred_element_type=F32)


def _silu(x):
    return x / (1.0 + jnp.exp(-x))


def _rms(x, g):
    return x * lax.rsqrt(jnp.mean(x * x, axis=-1, keepdims=True) + EPS) * g


def _layer_norm(x, g, b):
    mu = jnp.mean(x, axis=-1, keepdims=True)
    xc = x - mu
    var = jnp.mean(xc * xc, axis=-1, keepdims=True)
    return xc * lax.rsqrt(var + EPS) * g + b


def _mod_row(i, bm):
    row0 = i * bm
    return jnp.where(row0 < P_ROWS, 0, 1 + (row0 - P_ROWS) // DEC_SEQ)


def _mod_spec(layer, chunk, bm, lag=0):
    def index(i, *_):
        tile = jnp.maximum(i - lag, 0)
        return ((layer * MOD_ROWS + _mod_row(tile, bm)) * N_MOD + chunk, 0, 0)
    return pl.BlockSpec((1, 1, D), index)


def _row_vec_spec():
    return pl.BlockSpec((1, D), lambda i, *_: (0, 0))


def _mod_kernel(cv_ref, w_ref, b_ref, o_ref):
    k = pl.program_id(1)
    s = _silu(cv_ref[...]).astype(BF16)
    part = _dot(s, w_ref[...].astype(BF16))

    @pl.when(k == 0)
    def _():
        o_ref[...] = part + b_ref[...]

    @pl.when(k > 0)
    def _():
        o_ref[...] += part


def _modulation(cv, w_mod, b_mod):
    bk = 256
    nk = D // bk
    n = N_MOD * D
    cv_blocks = cv.reshape(MOD_ROWS, nk, bk).transpose(1, 0, 2)
    return pl.pallas_call(
        _mod_kernel,
        grid=(DEPTH, nk),
        in_specs=[
            pl.BlockSpec((None, MOD_ROWS, bk), lambda l, k: (k, 0, 0)),
            pl.BlockSpec((None, bk, n), lambda l, k: (l, k, 0)),
            pl.BlockSpec((None, 1, n), lambda l, k: (l, 0, 0)),
        ],
        out_specs=pl.BlockSpec((None, MOD_ROWS, n), lambda l, k: (l, 0, 0)),
        out_shape=jax.ShapeDtypeStruct((DEPTH, MOD_ROWS, n), F32),
        name="modulation",
        compiler_params=_params(("arbitrary", "arbitrary")),
    )(cv_blocks, w_mod, b_mod.reshape(DEPTH, 1, n))


def _norm_kernel(xp_ref, xs_ref, g_ref, sh_ref, sc_ref, x_ref, h_ref, *, p_tiles):
    def emit(src_ref):
        x = src_ref[...]
        x_ref[...] = x
        h = _rms(x, g_ref[...]) * (1.0 + sc_ref[0]) + sh_ref[0]
        h_ref[...] = h.astype(BF16)

    i = pl.program_id(0)
    pl.when(i < p_tiles)(lambda: emit(xp_ref))
    pl.when(i >= p_tiles)(lambda: emit(xs_ref))


def _first_norm(x_prompt, x_sample, g, mod):
    bm = 256
    p_tiles = P_ROWS // bm
    s_tiles = S_ROWS // bm
    return pl.pallas_call(
        functools.partial(_norm_kernel, p_tiles=p_tiles),
        grid=(ROWS // bm,),
        in_specs=[
            pl.BlockSpec((bm, D), lambda i: (jnp.minimum(i, p_tiles - 1), 0)),
            pl.BlockSpec((bm, D), lambda i: (jnp.clip(i - p_tiles, 0, s_tiles - 1), 0)),
            _row_vec_spec(),
            _mod_spec(0, 0, bm),
            _mod_spec(0, 1, bm),
        ],
        out_specs=[pl.BlockSpec((bm, D), lambda i: (i, 0)), pl.BlockSpec((bm, D), lambda i: (i, 0))],
        out_shape=[jax.ShapeDtypeStruct((ROWS, D), F32), jax.ShapeDtypeStruct((ROWS, D), BF16)],
        name="first_norm",
        compiler_params=_params(("arbitrary",)),
    )(x_prompt, x_sample, g, mod, mod)


def _cast_weights_once(pairs):
    @pl.when(pl.program_id(1) == 0)
    def _():
        for w_ref, w_scr in pairs:
            w_scr[...] = w_ref[...].astype(BF16)


def _w_spec(layer, bn, block0=0):
    return pl.BlockSpec((None, D, bn), lambda j, i: (layer, 0, block0 + j))


def _mm_kernel(h_ref, w_ref, o_ref, w_scr):
    _cast_weights_once([(w_ref, w_scr)])
    o_ref[...] = _dot(h_ref[...], w_scr[...]).astype(o_ref.dtype)


def _matmul(h, w, layer, n, bn, out_dtype):
    bm = BM_IN_WIDE
    return pl.pallas_call(
        _mm_kernel,
        grid=(n // bn, ROWS // bm),
        in_specs=[pl.BlockSpec((bm, D), lambda j, i: (i, 0)), _w_spec(layer, bn)],
        out_specs=pl.BlockSpec((bm, bn), lambda j, i: (i, j)),
        out_shape=jax.ShapeDtypeStruct((ROWS, n), out_dtype),
        scratch_shapes=[pltpu.VMEM((D, bn), BF16)],
        name="in_proj",
        compiler_params=_params(("arbitrary", "arbitrary")),
    )(h, w)


def _swiglu_kernel(h_ref, wa_ref, wb_ref, o_ref, wa_scr, wb_scr):
    _cast_weights_once([(wa_ref, wa_scr), (wb_ref, wb_scr)])
    h = h_ref[...]
    a = _dot(h, wa_scr[...])
    b = _dot(h, wb_scr[...])
    o_ref[...] = (_silu(a) * b).astype(o_ref.dtype)


def _ffn_in(h, w, layer):
    bm, bn = BM_IN, 512
    nb = D_FF // bn
    return pl.pallas_call(
        _swiglu_kernel,
        grid=(nb, ROWS // bm),
        in_specs=[pl.BlockSpec((bm, D), lambda j, i: (i, 0)), _w_spec(layer, bn), _w_spec(layer, bn, nb)],
        out_specs=pl.BlockSpec((bm, bn), lambda j, i: (i, j)),
        out_shape=jax.ShapeDtypeStruct((ROWS, D_FF), BF16),
        scratch_shapes=[pltpu.VMEM((D, bn), BF16), pltpu.VMEM((D, bn), BF16)],
        name="ffn_in",
        compiler_params=_params(("arbitrary", "arbitrary")),
    )(h, w, w)


def _glu_kernel(h_ref, wa_ref, wb_ref, ba_ref, bb_ref, o_ref, wa_scr, wb_scr):
    _cast_weights_once([(wa_ref, wa_scr), (wb_ref, wb_scr)])
    h = h_ref[...]
    a = _dot(h, wa_scr[...]) + ba_ref[...]
    gt = _dot(h, wb_scr[...]) + bb_ref[...]
    o_ref[...] = a * (1.0 / (1.0 + jnp.exp(-gt)))


def _pw1_glu(h, w, b, layer):
    bm, bn = BM_IN, 512
    nb = D // bn
    b2 = b.reshape(1, 2 * D)
    return pl.pallas_call(
        _glu_kernel,
        grid=(nb, ROWS // bm),
        in_specs=[
            pl.BlockSpec((bm, D), lambda j, i: (i, 0)),
            _w_spec(layer, bn),
            _w_spec(layer, bn, nb),
            pl.BlockSpec((1, bn), lambda j, i: (0, j)),
            pl.BlockSpec((1, bn), lambda j, i: (0, j + nb)),
        ],
        out_specs=pl.BlockSpec((bm, bn), lambda j, i: (i, j)),
        out_shape=jax.ShapeDtypeStruct((ROWS, D), F32),
        scratch_shapes=[pltpu.VMEM((D, bn), BF16), pltpu.VMEM((D, bn), BF16)],
        name="pw1_glu",
        compiler_params=_params(("arbitrary", "arbitrary")),
    )(h, w, w, b2, b2)


def _gelu_tanh(x):
    return 0.5 * x * (1.0 + jnp.tanh(0.7978845608028654 * (x + 0.044715 * (x * x * x))))


def _sg_kernel(h_ref, wu_ref, wv_ref, g_ref, b_ref, wsp_ref, bsp_ref, o_ref):
    h = h_ref[...]
    u = _gelu_tanh(_dot(h, wu_ref[...]))
    vv = _gelu_tanh(_dot(h, wv_ref[...]))
    vv = _layer_norm(vv, g_ref[0], b_ref[0]).astype(BF16)
    wsp = wsp_ref[0]
    bsp = bsp_ref[0]
    for c in range(h.shape[0] // SG_CHUNK):
        rows = slice(c * SG_CHUNK, (c + 1) * SG_CHUNK)
        mixed = _dot(wsp, vv[rows]) + bsp
        o_ref[rows, :] = (u[rows] * mixed).astype(o_ref.dtype)


def _spatial_gating(h, w_sg, sg_g, sg_b, w_sp, b_sp):
    bm = BM_IN
    return pl.pallas_call(
        _sg_kernel,
        grid=(ROWS // bm, SG_GROUPS),
        in_specs=[
            pl.BlockSpec((bm, D), lambda i, g: (i, 0)),
            pl.BlockSpec((D, SG_CH), lambda i, g: (0, g)),
            pl.BlockSpec((D, SG_CH), lambda i, g: (0, g + SG_GROUPS)),
            pl.BlockSpec((1, 1, SG_CH), lambda i, g: (g, 0, 0)),
            pl.BlockSpec((1, 1, SG_CH), lambda i, g: (g, 0, 0)),
            pl.BlockSpec((1, SG_CHUNK, SG_CHUNK), lambda i, g: (g, 0, 0)),
            pl.BlockSpec((1, SG_CHUNK, 1), lambda i, g: (g, 0, 0)),
        ],
        out_specs=pl.BlockSpec((bm, SG_CH), lambda i, g: (i, g)),
        out_shape=jax.ShapeDtypeStruct((ROWS, SG_W), BF16),
        name="spatial_gating",
        compiler_params=_params(("arbitrary", "arbitrary")),
    )(h, w_sg, w_sg, sg_g.reshape(SG_GROUPS, 1, SG_CH), sg_b.reshape(SG_GROUPS, 1, SG_CH),
      w_sp.astype(BF16), b_sp.reshape(SG_GROUPS, SG_CHUNK, 1))


def _chunk_triangles():
    i = jnp.arange(GLA_ROWS)[:, None]
    j = jnp.arange(GLA_ROWS)[None, :]
    same = (i // CHUNK) == (j // CHUNK)
    return jnp.stack([same & (j <= i), same & (j >= i)]).astype(BF16)


def _gla_kernel(*refs, seq, has_init, emit_state):
    q_ref, k_ref, v_ref, r_ref, g_ref, wg_ref, bg_ref, gg_ref, tri_ref = refs[:9]
    pos = 9
    s0_ref = None
    if has_init:
        s0_ref = refs[pos]
        pos += 1
    o_ref = refs[pos]
    pos += 1
    sfin_ref = None
    if emit_state:
        sfin_ref = refs[pos]
        pos += 1
    b_scr, of_scr, ob_scr = refs[pos:pos + 3]

    n_steps = seq // GLA_ROWS
    n_sub = GLA_ROWS // CHUNK
    scale = DK ** -0.5

    def decay_step(s, carry):
        rows = pl.ds(pl.multiple_of(s * GLA_ROWS, GLA_ROWS), GLA_ROWS)
        logits = _dot(g_ref[rows, :].astype(BF16), wg_ref[0]) + bg_ref[0]
        la = (jnp.minimum(logits, 0.0) - jnp.log1p(jnp.exp(-jnp.abs(logits)))) / TAU
        hi = la.astype(BF16)
        lo = (la - hi.astype(F32)).astype(BF16)
        tri = tri_ref[0]
        pre = _dot(tri, hi) + _dot(tri, lo)
        b_scr[rows, 0:DK] = pre[:, 0:DK]
        for c in range(n_sub):
            cs = slice(c * CHUNK, (c + 1) * CHUNK)
            total = pre[(c + 1) * CHUNK - 1:(c + 1) * CHUNK, DK:]
            b_scr[pl.ds(s * GLA_ROWS + c * CHUNK, CHUNK), DK:2 * DK] = total - pre[cs, DK:] + la[cs, DK:]
        return carry

    unroll = 2 if n_steps % 2 == 0 else 1
    lax.fori_loop(0, n_steps, decay_step, 0, unroll=unroll)

    def rows_step(r0, direction, st, out_scr):
        rows = pl.ds(r0, GLA_ROWS)
        b = b_scr[rows, direction * DK:(direction + 1) * DK]
        last = [(c + 1) * CHUNK - 1 if direction == 0 else c * CHUNK for c in range(n_sub)]
        rem = jnp.concatenate(
            [b[last[c]:last[c] + 1, :] - b[c * CHUNK:(c + 1) * CHUNK, :] for c in range(n_sub)], axis=0)
        q = q_ref[rows, :] * scale
        k = k_ref[rows, :]
        v = v_ref[rows, :].astype(BF16)
        q_in = (q * jnp.exp(b)).astype(BF16)
        k_in = (k * jnp.exp(-b)).astype(BF16)
        k_out = (k * jnp.exp(rem)).astype(BF16)
        mask = tri_ref[direction] > 0
        att = jnp.where(mask, _dot_nt(q_in, k_in), 0.0).astype(BF16)
        o_intra = _dot(att, v)
        order = range(n_sub) if direction == 0 else range(n_sub - 1, -1, -1)
        for c in order:
            cs = slice(c * CHUNK, (c + 1) * CHUNK)
            out_scr[pl.ds(r0 + c * CHUNK, CHUNK), :] = o_intra[cs] + _dot_nt(q_in[cs], st.astype(BF16))
            st = st * jnp.exp(b[last[c]:last[c] + 1, :]) + _dot_tn(v[cs], k_out[cs])
        return st

    def body(s, carry):
        st_f, st_b = carry
        rf = pl.multiple_of(s * GLA_ROWS, GLA_ROWS)
        rb = pl.multiple_of((n_steps - 1 - s) * GLA_ROWS, GLA_ROWS)
        st_f = rows_step(rf, 0, st_f, of_scr)
        st_b = rows_step(rb, 1, st_b, ob_scr)
        return st_f, st_b

    if has_init:
        init = (s0_ref[0].T, s0_ref[1].T)
    else:
        init = (jnp.zeros((DV, DK), F32), jnp.zeros((DV, DK), F32))
    st_f, st_b = lax.fori_loop(0, n_steps, body, init, unroll=unroll)

    if emit_state:
        sfin_ref[0] = st_f.T
        sfin_ref[1] = st_b.T

    gg = gg_ref[0]

    def finish(i, carry):
        rows = pl.ds(pl.multiple_of(i * GLA_ROWS, GLA_ROWS), GLA_ROWS)
        o = _rms(of_scr[rows, :] + ob_scr[rows, :], gg)
        o_ref[rows, :] = (o * _silu(r_ref[rows, :])).astype(o_ref.dtype)
        return carry

    lax.fori_loop(0, n_steps, finish, 0, unroll=unroll)


def _gla(z1, wg, bg, gla_g, tri, seq, row_block0, n_seq, s0=None, layer_idx=0, emit_state=False):
    qk_blocks = QK_W // DK
    v_block0 = 2 * QK_W // DV
    r_block0 = v_block0 + HEADS
    g_block = (2 * QK_W + 2 * V_W) // DK
    rb = row_block0
    in_specs = [
        pl.BlockSpec((seq, DK), lambda b, h: (rb + b, h)),
        pl.BlockSpec((seq, DK), lambda b, h: (rb + b, qk_blocks + h)),
        pl.BlockSpec((seq, DV), lambda b, h: (rb + b, v_block0 + h)),
        pl.BlockSpec((seq, DV), lambda b, h: (rb + b, r_block0 + h)),
        pl.BlockSpec((seq, DK), lambda b, h: (rb + b, g_block)),
        pl.BlockSpec((1, DK, 2 * DK), lambda b, h: (h, 0, 0)),
        pl.BlockSpec((1, 1, 2 * DK), lambda b, h: (h, 0, 0)),
        pl.BlockSpec((1, 1, DV), lambda b, h: (h, 0, 0)),
        pl.BlockSpec((2, GLA_ROWS, GLA_ROWS), lambda b, h: (0, 0, 0)),
    ]
    args = [z1, z1, z1, z1, z1, wg, bg, gla_g.reshape(HEADS, 1, DV), tri]
    if s0 is not None:
        in_specs.append(pl.BlockSpec((None, None, 2, None, DK, DV),
                                     lambda b, h: (b, layer_idx, 0, h, 0, 0)))
        args.append(s0)
    out_specs = [pl.BlockSpec((seq, DV), lambda b, h: (b, h))]
    out_shape = [jax.ShapeDtypeStruct((n_seq * seq, V_W), BF16)]
    if emit_state:
        out_specs.append(pl.BlockSpec((None, 2, None, DK, DV), lambda b, h: (b, 0, h, 0, 0)))
        out_shape.append(jax.ShapeDtypeStruct((n_seq, 2, HEADS, DK, DV), F32))
    kern = functools.partial(_gla_kernel, seq=seq, has_init=s0 is not None, emit_state=emit_state)
    return pl.pallas_call(
        kern,
        grid=(n_seq, HEADS),
        in_specs=in_specs,
        out_specs=out_specs,
        out_shape=out_shape,
        scratch_shapes=[
            pltpu.VMEM((seq, 2 * DK), F32),
            pltpu.VMEM((seq, DV), F32),
            pltpu.VMEM((seq, DV), F32),
        ],
        name="gla_seq%d" % seq,
        compiler_params=_params(("arbitrary", "arbitrary")),
    )(*args)


def _ln_silu_rows(src_ref, g_ref, b_ref, o_ref):
    def step(r, carry):
        rows = pl.ds(pl.multiple_of(r * EPI_ROWS, EPI_ROWS), EPI_ROWS)
        y = _layer_norm(src_ref[rows, :], g_ref[...], b_ref[...])
        o_ref[rows, :] = _silu(y).astype(o_ref.dtype)
        return carry

    lax.fori_loop(0, o_ref.shape[0] // EPI_ROWS, step, 0, unroll=2)


def _conv_rows_kernel(z_ref, w_ref, b_ref, lg_ref, lb_ref, o_ref, pad_scr, conv_scr, *, seg):
    rows = z_ref.shape[0]
    n_seg = rows // seg
    plen = seg + 2 * CONV_HALO
    zeros = jnp.zeros((CONV_HALO, LANES), F32)
    first = CONV_HALO - HALF_K
    for ct in range(D // LANES):
        cols = slice(ct * LANES, (ct + 1) * LANES)
        for s in range(n_seg):
            base = s * plen
            pad_scr[ct, base:base + CONV_HALO, :] = zeros
            pad_scr[ct, base + CONV_HALO:base + CONV_HALO + seg, :] = z_ref[s * seg:(s + 1) * seg, cols]
            pad_scr[ct, base + CONV_HALO + seg:base + plen, :] = zeros
        acc = [jnp.zeros((seg, LANES), F32) + b_ref[:, cols] for _ in range(n_seg)]
        for t in range(CONV_K):
            tap = w_ref[t:t + 1, cols]
            for s in range(n_seg):
                start = s * plen + first + t
                acc[s] = acc[s] + pad_scr[ct, start:start + seg, :] * tap
        for s in range(n_seg):
            conv_scr[s * seg:(s + 1) * seg, cols] = acc[s]

    _ln_silu_rows(conv_scr, lg_ref, lb_ref, o_ref)


def _conv_rows(z, w, b, ln_g, ln_b, row0, n_rows, seg):
    rows = CONV_ROWS
    rb0 = row0 // rows
    plen = seg + 2 * CONV_HALO
    return pl.pallas_call(
        functools.partial(_conv_rows_kernel, seg=seg),
        grid=(n_rows // rows,),
        in_specs=[
            pl.BlockSpec((rows, D), lambda i: (rb0 + i, 0)),
            pl.BlockSpec((CONV_K, D), lambda i: (0, 0)),
            _row_vec_spec(),
            _row_vec_spec(),
            _row_vec_spec(),
        ],
        out_specs=pl.BlockSpec((rows, D), lambda i: (i, 0)),
        out_shape=jax.ShapeDtypeStruct((n_rows, D), BF16),
        scratch_shapes=[
            pltpu.VMEM((D // LANES, (rows // seg) * plen, LANES), F32),
            pltpu.VMEM((rows, D), F32),
        ],
        name="conv_rows_seg%d" % seg,
        compiler_params=_params(("arbitrary",)),
    )(z, w, b.reshape(1, D), ln_g.reshape(1, D), ln_b.reshape(1, D))


def _conv_cols_kernel(z_ref, w_ref, b_ref, o_ref, pad_scr):
    rows, ct = z_ref.shape
    halo = HALF_K * GRID_W
    zeros = jnp.zeros((halo, ct), F32)
    pad_scr[0:halo, :] = zeros
    pad_scr[halo:halo + rows, :] = z_ref[...]
    pad_scr[halo + rows:2 * halo + rows, :] = zeros
    blk = 256

    def body(i, carry):
        r0 = pl.multiple_of(i * blk, blk)
        acc = jnp.zeros((blk, ct), F32) + b_ref[...]
        for t in range(CONV_K):
            acc = acc + pad_scr[pl.ds(r0 + t * GRID_W, blk), :] * w_ref[t:t + 1, :]
        o_ref[pl.ds(r0, blk), :] = acc
        return carry

    lax.fori_loop(0, rows // blk, body, 0)


def _conv_cols(z, w, b, row0, n_rows, ct):
    rows = DEC_SEQ
    rb0 = row0 // rows
    return pl.pallas_call(
        _conv_cols_kernel,
        grid=(n_rows // rows, D // ct),
        in_specs=[
            pl.BlockSpec((rows, ct), lambda i, j: (rb0 + i, j)),
            pl.BlockSpec((CONV_K, ct), lambda i, j: (0, j)),
            pl.BlockSpec((1, ct), lambda i, j: (0, j)),
        ],
        out_specs=pl.BlockSpec((rows, ct), lambda i, j: (i, j)),
        out_shape=jax.ShapeDtypeStruct((n_rows, D), F32),
        scratch_shapes=[pltpu.VMEM((rows + 2 * HALF_K * GRID_W, ct), F32)],
        name="conv_cols",
        compiler_params=_params(("arbitrary", "arbitrary")),
    )(z, w, b.reshape(1, D))


def _ln_silu_kernel(y_ref, g_ref, b_ref, o_ref):
    _ln_silu_rows(y_ref, g_ref, b_ref, o_ref)


def _ln_silu(y, ln_g, ln_b):
    bm = 512
    n_rows = y.shape[0]
    return pl.pallas_call(
        _ln_silu_kernel,
        grid=(n_rows // bm,),
        in_specs=[pl.BlockSpec((bm, D), lambda i: (i, 0)), _row_vec_spec(), _row_vec_spec()],
        out_specs=pl.BlockSpec((bm, D), lambda i: (i, 0)),
        out_shape=jax.ShapeDtypeStruct((n_rows, D), BF16),
        name="ln_silu",
        compiler_params=_params(("arbitrary",)),
    )(y, ln_g.reshape(1, D), ln_b.reshape(1, D))


def _out_kernel(*refs, terms, nj, bn, p_tiles, has_bias, emit_h, split_out):
    pos = 0
    lhs = []
    for split in terms:
        n = 2 if split else 1
        lhs.append(refs[pos:pos + n])
        pos += n
    w_refs = refs[pos:pos + len(terms)]
    pos += len(terms)
    if has_bias:
        bias_ref = refs[pos]
        pos += 1
    x_ref, ga_ref, gate_ref = refs[pos:pos + 3]
    pos += 3
    if emit_h:
        gb_ref, sh_ref, sc_ref = refs[pos:pos + 3]
        pos += 3
    n_out = 2 if split_out else 1
    xo_refs = refs[pos:pos + n_out]
    pos += n_out
    if emit_h:
        h_ref = refs[pos]
        pos += 1
    y_scr, vec_scr = refs[pos:pos + 2]

    i = pl.program_id(0)
    j = pl.program_id(1)

    def column_tile(latent):
        y = None
        for l, w in zip(lhs, w_refs):
            a = l[1 if (latent and len(l) == 2) else 0][...]
            d = _dot(a, w[...])
            y = d if y is None else y + d
        y_scr[j] = y

    if any(terms):
        pl.when(i < p_tiles)(lambda: column_tile(False))
        pl.when(i >= p_tiles)(lambda: column_tile(True))
    else:
        column_tile(False)

    col_slices = [slice(jj * bn, (jj + 1) * bn) for jj in range(nj)]

    def finish(xo_ref):
        vec_scr[0] = gate_ref[0] * ga_ref[...]
        if emit_h:
            vec_scr[1] = gb_ref[...] * (1.0 + sc_ref[0])

        def rows_step(r, carry):
            rows = pl.ds(pl.multiple_of(r * EPI_ROWS, EPI_ROWS), EPI_ROWS)
            ys = []
            ssq = None
            for jj, cols in enumerate(col_slices):
                y = y_scr[jj, rows, :]
                if has_bias:
                    y = y + bias_ref[:, cols]
                ys.append(y)
                part = jnp.sum(y * y, axis=-1, keepdims=True)
                ssq = part if ssq is None else ssq + part
            inv = lax.rsqrt(ssq * (1.0 / D) + EPS)
            xs = []
            ssq = None
            for jj, cols in enumerate(col_slices):
                x_new = x_ref[rows, cols] + ys[jj] * inv * vec_scr[0, :, cols]
                xo_ref[rows, cols] = x_new
                if emit_h:
                    xs.append(x_new)
                    part = jnp.sum(x_new * x_new, axis=-1, keepdims=True)
                    ssq = part if ssq is None else ssq + part
            if emit_h:
                inv = lax.rsqrt(ssq * (1.0 / D) + EPS)
                for jj, cols in enumerate(col_slices):
                    h = xs[jj] * inv * vec_scr[1, :, cols] + sh_ref[0, :, cols]
                    h_ref[rows, cols] = h.astype(BF16)
            return carry

        lax.fori_loop(0, xo_ref.shape[0] // EPI_ROWS, rows_step, 0, unroll=2)

    last = j == nj - 1
    if split_out:
        pl.when(last & (i < p_tiles))(lambda: finish(xo_refs[0]))
        pl.when(last & (i >= p_tiles))(lambda: finish(xo_refs[1]))
    else:
        pl.when(last)(lambda: finish(xo_refs[0]))


def _out_proj(terms, w, layer_w, x, mod, g_y, gate_idx, next_norm, *, bn, bias=None, split_out=False):
    bm = BM_OUT
    nj = D // bn
    p_tiles = P_ROWS // bm
    s_tiles = S_ROWS // bm
    emit_h = next_norm is not None
    in_specs, args = [], []
    for arrays, _, kt in terms:
        if len(arrays) == 2:
            in_specs += [
                pl.BlockSpec((bm, kt), lambda i, j: (jnp.minimum(i, p_tiles - 1), 0)),
                pl.BlockSpec((bm, kt), lambda i, j: (jnp.clip(i - p_tiles, 0, s_tiles - 1), 0)),
            ]
        else:
            in_specs.append(pl.BlockSpec((bm, kt), lambda i, j: (i, 0)))
        args += list(arrays)
    for _, k0, kt in terms:
        kb = k0 // kt
        if w.ndim == 4:
            in_specs.append(pl.BlockSpec((None, None, kt, bn), lambda i, j, kb=kb: (layer_w, j, kb, 0)))
        else:
            in_specs.append(pl.BlockSpec((None, kt, bn), lambda i, j, kb=kb: (layer_w, kb, j)))
        args.append(w)
    if bias is not None:
        in_specs.append(_row_vec_spec())
        args.append(bias.reshape(1, D))
    in_specs += [pl.BlockSpec((bm, D), lambda i, j: (i, 0)),
                 _row_vec_spec(), _mod_spec(gate_idx[0], gate_idx[1], bm)]
    args += [x, g_y.reshape(1, D), mod]
    if split_out:
        out_specs = [pl.BlockSpec((bm, D), lambda i, j: (jnp.minimum(i, p_tiles - 1), 0)),
                     pl.BlockSpec((bm, D), lambda i, j: (jnp.clip(i - p_tiles, 0, s_tiles - 1), 0))]
        out_shape = [jax.ShapeDtypeStruct((P_ROWS, D), F32), jax.ShapeDtypeStruct((S_ROWS, D), F32)]
    else:
        out_specs = [pl.BlockSpec((bm, D), lambda i, j: (i, 0))]
        out_shape = [jax.ShapeDtypeStruct((ROWS, D), F32)]
    if emit_h:
        gain, (layer, shift_chunk, scale_chunk) = next_norm
        in_specs += [_row_vec_spec(), _mod_spec(layer, shift_chunk, bm), _mod_spec(layer, scale_chunk, bm)]
        args += [gain.reshape(1, D), mod, mod]
        out_specs.append(pl.BlockSpec((bm, D), lambda i, j: (i, 0)))
        out_shape.append(jax.ShapeDtypeStruct((ROWS, D), BF16))
    kern = functools.partial(_out_kernel, terms=tuple(len(t[0]) == 2 for t in terms), nj=nj, bn=bn,
                             p_tiles=p_tiles, has_bias=bias is not None, emit_h=emit_h, split_out=split_out)
    res = pl.pallas_call(
        kern,
        grid=(ROWS // bm, nj),
        in_specs=in_specs,
        out_specs=out_specs,
        out_shape=out_shape,
        scratch_shapes=[pltpu.VMEM((nj, bm, bn), F32), pltpu.VMEM((2, 1, D), F32)],
        name="out_proj_k%d" % sum(t[2] for t in terms),
        compiler_params=_params(("arbitrary", "arbitrary")),
    )(*args)
    if split_out:
        return (res[0], res[1]), None
    return (res[0], res[1]) if emit_h else (res[0], None)


def _out_lagged_kernel(*refs, terms, n_tiles, p_tiles, has_bias):
    pos = 0
    lhs = []
    for split in terms:
        n = 2 if split else 1
        lhs.append(refs[pos:pos + n])
        pos += n
    w_refs = refs[pos:pos + len(terms)]
    pos += len(terms)
    if has_bias:
        bias_ref = refs[pos]
        pos += 1
    x_ref, ga_ref, gate_ref, gb_ref, sh_ref, sc_ref, xo_ref, h_ref, y_even, y_odd, vec_scr = refs[pos:pos + 11]

    i = pl.program_id(0)
    bm = xo_ref.shape[0]

    @pl.when(i == 0)
    def _():
        y_odd[...] = jnp.zeros_like(y_odd)

    def step(y_write, y_read):
        vec_scr[0] = gate_ref[0] * ga_ref[...]
        vec_scr[1] = gb_ref[...] * (1.0 + sc_ref[0])
        for r in range(bm // LAG_ROWS):
            rows = slice(r * LAG_ROWS, (r + 1) * LAG_ROWS)
            yr = y_read[rows, :]
            if has_bias:
                yr = yr + bias_ref[...]
            inv = lax.rsqrt(jnp.sum(yr * yr, axis=-1, keepdims=True) * (1.0 / D) + EPS)
            x_new = x_ref[rows, :] + yr * inv * vec_scr[0]
            xo_ref[rows, :] = x_new
            inv = lax.rsqrt(jnp.sum(x_new * x_new, axis=-1, keepdims=True) * (1.0 / D) + EPS)
            h_ref[rows, :] = (x_new * inv * vec_scr[1] + sh_ref[0]).astype(BF16)

        y = None
        for l, w in zip(lhs, w_refs):
            a = jnp.where(i < p_tiles, l[0][...], l[1][...]) if len(l) == 2 else l[0][...]
            d = _dot(a, w[...])
            y = d if y is None else y + d
        y_write[...] = y

    pl.when(i % 2 == 0)(lambda: step(y_even, y_odd))
    pl.when(i % 2 == 1)(lambda: step(y_odd, y_even))


def _out_proj_lagged(terms, w, layer_w, x, mod, g_y, gate_idx, next_norm, *, bias=None):
    bm = BM_OUT
    n_tiles = ROWS // bm
    p_tiles = P_ROWS // bm
    s_tiles = S_ROWS // bm
    in_specs, args = [], []
    for arrays, _, kt in terms:
        if len(arrays) == 2:
            in_specs += [
                pl.BlockSpec((bm, kt), lambda i: (jnp.minimum(i, p_tiles - 1), 0)),
                pl.BlockSpec((bm, kt), lambda i: (jnp.clip(i - p_tiles, 0, s_tiles - 1), 0)),
            ]
        else:
            in_specs.append(pl.BlockSpec((bm, kt), lambda i: (jnp.minimum(i, n_tiles - 1), 0)))
        args += list(arrays)
    for _, k0, kt in terms:
        in_specs.append(pl.BlockSpec((None, kt, D), lambda i, kb=k0 // kt: (layer_w, kb, 0)))
        args.append(w)
    if bias is not None:
        in_specs.append(_row_vec_spec())
        args.append(bias.reshape(1, D))
    prev = lambda i: (jnp.maximum(i - 1, 0), 0)
    gain, (layer, shift_chunk, scale_chunk) = next_norm
    in_specs += [pl.BlockSpec((bm, D), prev), _row_vec_spec(), _mod_spec(gate_idx[0], gate_idx[1], bm, lag=1),
                 _row_vec_spec(), _mod_spec(layer, shift_chunk, bm, lag=1), _mod_spec(layer, scale_chunk, bm, lag=1)]
    args += [x, g_y.reshape(1, D), mod, gain.reshape(1, D), mod, mod]
    kern = functools.partial(_out_lagged_kernel, terms=tuple(len(t[0]) == 2 for t in terms),
                             n_tiles=n_tiles, p_tiles=p_tiles, has_bias=bias is not None)
    x_new, h = pl.pallas_call(
        kern,
        grid=(n_tiles + 1,),
        in_specs=in_specs,
        out_specs=[pl.BlockSpec((bm, D), prev), pl.BlockSpec((bm, D), prev)],
        out_shape=[jax.ShapeDtypeStruct((ROWS, D), F32), jax.ShapeDtypeStruct((ROWS, D), BF16)],
        scratch_shapes=[pltpu.VMEM((bm, D), F32), pltpu.VMEM((bm, D), F32), pltpu.VMEM((2, 1, D), F32)],
        name="out_proj_lagged_k%d" % sum(t[2] for t in terms),
        compiler_params=_params(("arbitrary",)),
    )(*args)
    return x_new, h


def kernel(x_prompt, x_sample, state_gla, c, c_ctx, norm_g, w_mod, b_mod, w_in_a, w_gate_up, b_gate, gla_norm_g, sg_norm_g, sg_norm_b, w_spatial, b_spatial, w_out_a, w_pw1, b_pw1, w_dw, b_dw, conv_norm_g, conv_norm_b, w_pw2, b_pw2, w_ffn_in, w_ffn_out):
    cv = jnp.concatenate([c_ctx[None, :], c, jnp.zeros((MOD_ROWS - 1 - DEC_BATCH, D), F32)], axis=0)
    mod = _modulation(cv, w_mod, b_mod).reshape(DEPTH * MOD_ROWS * N_MOD, 1, D)
    tri = _chunk_triangles()
    n_z1 = 2 * QK_W + 2 * V_W + 2 * RANK
    w_z1 = jnp.pad(w_in_a[:, :, :n_z1], ((0, 0), (0, 0), (0, Z1_W - n_z1))).astype(BF16)
    n_col = D // FFN_OUT_BN
    w_out_bf = w_out_a.astype(BF16)
    w_pw2_bf = w_pw2.astype(BF16)
    w_ffn_out_bf = w_ffn_out.astype(BF16).reshape(DEPTH, D_FF, n_col, FFN_OUT_BN).transpose(0, 2, 1, 3)

    x, h = _first_norm(x_prompt.reshape(P_ROWS, D), x_sample.reshape(S_ROWS, D),
                       norm_g[0, 0].reshape(1, D), mod)
    new_states = []
    for layer in range(DEPTH):
        g = norm_g[layer]
        i = layer // 2
        mid_norm = (g[2], (layer, 3, 4))
        if layer % 2 == 0:
            w_sg = w_in_a[i, :, n_z1:].astype(BF16)
            z1 = _matmul(h, w_z1, i, Z1_W, 640, F32)
            sg_out = _spatial_gating(h, w_sg, sg_norm_g[i], sg_norm_b[i], w_spatial[i], b_spatial[i])

            wf = w_gate_up[i, 0].reshape(RANK, HEADS, DK).transpose(1, 0, 2)
            wb = w_gate_up[i, 1].reshape(RANK, HEADS, DK).transpose(1, 0, 2)
            wg = jnp.zeros((HEADS, DK, 2 * DK), F32)
            wg = wg.at[:, 0:RANK, 0:DK].set(wf).at[:, RANK:2 * RANK, DK:2 * DK].set(wb).astype(BF16)
            bg = jnp.concatenate([b_gate[i, 0].reshape(HEADS, 1, DK), b_gate[i, 1].reshape(HEADS, 1, DK)],
                                 axis=-1)
            o_p, s_new = _gla(z1, wg, bg, gla_norm_g[i], tri, SEQ, 0, BATCH, emit_state=True)
            (o_s,) = _gla(z1, wg, bg, gla_norm_g[i], tri, DEC_SEQ, P_ROWS // DEC_SEQ, DEC_BATCH,
                          s0=state_gla, layer_idx=i)
            new_states.append(s_new)

            x, h = _out_proj_lagged([((o_p, o_s), 0, V_W), ((sg_out,), V_W, SG_W)], w_out_bf, i,
                                    x, mod, g[1], (layer, 2), mid_norm)
        else:
            z = _pw1_glu(h, w_pw1, b_pw1[i], i)
            ln = (conv_norm_g[i], conv_norm_b[i])
            conv_p = _conv_rows(z, w_dw[i], b_dw[i], *ln, 0, P_ROWS, SEQ)
            if i % 2 == 1:
                conv_s = _ln_silu(_conv_cols(z, w_dw[i], b_dw[i], P_ROWS, S_ROWS, 128), *ln)
            else:
                conv_s = _conv_rows(z, w_dw[i], b_dw[i], *ln, P_ROWS, S_ROWS, GRID_W)
            x, h = _out_proj_lagged([((conv_p, conv_s), 0, D)], w_pw2_bf, i,
                                    x, mod, g[1], (layer, 2), mid_norm, bias=b_pw2[i])

        act = _ffn_in(h, w_ffn_in, layer)
        final = layer == DEPTH - 1
        next_norm = None if final else (norm_g[layer + 1, 0], (layer + 1, 0, 1))
        x, h = _out_proj([((act,), 0, D_FF)], w_ffn_out_bf, layer,
                         x, mod, g[3], (layer, 5), next_norm, bn=FFN_OUT_BN, split_out=final)

    y_prompt = x[0].reshape(BATCH, SEQ, D)
    y_sample = x[1].reshape(DEC_BATCH, DEC_SEQ, D)
    state_new = jnp.stack(new_states, axis=1).astype(x_prompt.dtype)
    return (y_prompt, y_sample, state_new)
```

```python
import functools

import jax
import jax.numpy as jnp
from jax import lax
from jax.experimental import pallas as pl
from jax.experimental.pallas import tpu as pltpu

F32 = jnp.float32
BF16 = jnp.bfloat16

D = 2048
BATCH = 16
SEQ = 256
DEPTH = 4
DEC_BATCH = 4
DEC_SEQ = 2048
GRID_W = 64
HEADS = 4
DK = 128
DV = 256
RANK = 16
TAU = 16.0
CHUNK = 64
SG_GROUPS = 4
SG_CH = 256
SG_CHUNK = 128
QK_W = HEADS * DK
V_W = HEADS * DV
SG_W = SG_GROUPS * SG_CH
CONV_K = 31
HALF_K = CONV_K // 2
D_FF = 5632
N_MOD = 6
EPS = 1e-6

P_ROWS = BATCH * SEQ
S_ROWS = DEC_BATCH * DEC_SEQ
ROWS = P_ROWS + S_ROWS
MOD_ROWS = 8
Z1_W = 2 * QK_W + 2 * V_W + 128

LANES = 128
SUBLANES = 8
BM_IN = 1024
BM_IN_WIDE = 2048
BM_OUT = 512
FFN_OUT_BN = 512
EPI_ROWS = 64
LAG_ROWS = 16
GLA_ROWS = 4 * CHUNK
CONV_ROWS = 256
CONV_HALO = 16
VMEM_LIMIT = 56 * 1024 * 1024


def _params(sem):
    return pltpu.CompilerParams(dimension_semantics=sem, vmem_limit_bytes=VMEM_LIMIT)


def _dot(a, b):
    return jnp.dot(a, b, preferred_element_type=F32)


def _dot_nt(a, b):
    return lax.dot_general(a, b, (((1,), (1,)), ((), ())), preferred_element_type=F32)


def _dot_tn(a, b):
    return lax.dot_general(a, b, (((0,), (0,)), ((), ())), preferred_element_type=F32)


def _silu(x):
    return x / (1.0 + jnp.exp(-x))


def _rms(x, g):
    return x * lax.rsqrt(jnp.mean(x * x, axis=-1, keepdims=True) + EPS) * g


def _layer_norm(x, g, b):
    mu = jnp.mean(x, axis=-1, keepdims=True)
    xc = x - mu
    var = jnp.mean(xc * xc, axis=-1, keepdims=True)
    return xc * lax.rsqrt(var + EPS) * g + b


def _mod_row(i, bm):
    row0 = i * bm
    return jnp.where(row0 < P_ROWS, 0, 1 + (row0 - P_ROWS) // DEC_SEQ)


def _mod_spec(layer, chunk, bm, lag=0):
    def index(i, *_):
        tile = jnp.maximum(i - lag, 0)
        return ((layer * MOD_ROWS + _mod_row(tile, bm)) * N_MOD + chunk, 0, 0)
    return pl.BlockSpec((1, 1, D), index)


def _row_vec_spec():
    return pl.BlockSpec((1, D), lambda i, *_: (0, 0))


def _mod_kernel(cv_ref, w_ref, b_ref, o_ref):
    k = pl.program_id(1)
    s = _silu(cv_ref[...]).astype(BF16)
    part = _dot(s, w_ref[...].astype(BF16))

    @pl.when(k == 0)
    def _():
        o_ref[...] = part + b_ref[...]

    @pl.when(k > 0)
    def _():
        o_ref[...] += part


def _modulation(cv, w_mod, b_mod):
    bk = 256
    nk = D // bk
    n = N_MOD * D
    cv_blocks = cv.reshape(MOD_ROWS, nk, bk).transpose(1, 0, 2)
    return pl.pallas_call(
        _mod_kernel,
        grid=(DEPTH, nk),
        in_specs=[
            pl.BlockSpec((None, MOD_ROWS, bk), lambda l, k: (k, 0, 0)),
            pl.BlockSpec((None, bk, n), lambda l, k: (l, k, 0)),
            pl.BlockSpec((None, 1, n), lambda l, k: (l, 0, 0)),
        ],
        out_specs=pl.BlockSpec((None, MOD_ROWS, n), lambda l, k: (l, 0, 0)),
        out_shape=jax.ShapeDtypeStruct((DEPTH, MOD_ROWS, n), F32),
        name="modulation",
        compiler_params=_params(("arbitrary", "arbitrary")),
    )(cv_blocks, w_mod, b_mod.reshape(DEPTH, 1, n))


def _norm_kernel(xp_ref, xs_ref, g_ref, sh_ref, sc_ref, x_ref, h_ref, *, p_tiles):
    def emit(src_ref):
        def step(r, carry):
            rows = pl.ds(pl.multiple_of(r * EPI_ROWS, EPI_ROWS), EPI_ROWS)
            x = src_ref[rows, :]
            x_ref[rows, :] = x
            h = _rms(x, g_ref[...]) * (1.0 + sc_ref[0]) + sh_ref[0]
            h_ref[rows, :] = h.astype(BF16)
            return carry

        lax.fori_loop(0, x_ref.shape[0] // EPI_ROWS, step, 0, unroll=2)

    i = pl.program_id(0)
    pl.when(i < p_tiles)(lambda: emit(xp_ref))
    pl.when(i >= p_tiles)(lambda: emit(xs_ref))


def _first_norm(x_prompt, x_sample, g, mod):
    bm = 256
    p_tiles = P_ROWS // bm
    s_tiles = S_ROWS // bm
    return pl.pallas_call(
        functools.partial(_norm_kernel, p_tiles=p_tiles),
        grid=(ROWS // bm,),
        in_specs=[
            pl.BlockSpec((bm, D), lambda i: (jnp.minimum(i, p_tiles - 1), 0)),
            pl.BlockSpec((bm, D), lambda i: (jnp.clip(i - p_tiles, 0, s_tiles - 1), 0)),
            _row_vec_spec(),
            _mod_spec(0, 0, bm),
            _mod_spec(0, 1, bm),
        ],
        out_specs=[pl.BlockSpec((bm, D), lambda i: (i, 0)), pl.BlockSpec((bm, D), lambda i: (i, 0))],
        out_shape=[jax.ShapeDtypeStruct((ROWS, D), F32), jax.ShapeDtypeStruct((ROWS, D), BF16)],
        name="first_norm",
        compiler_params=_params(("arbitrary",)),
    )(x_prompt, x_sample, g, mod, mod)


def _cast_weights_once(pairs):
    @pl.when(pl.program_id(1) == 0)
    def _():
        for w_ref, w_scr in pairs:
            w_scr[...] = w_ref[...].astype(BF16)


def _w_spec(layer, bn, block0=0):
    return pl.BlockSpec((None, D, bn), lambda j, i: (layer, 0, block0 + j))


def _mm_kernel(h_ref, w_ref, o_ref, w_scr):
    _cast_weights_once([(w_ref, w_scr)])
    o_ref[...] = _dot(h_ref[...], w_scr[...]).astype(o_ref.dtype)


def _matmul(h, w, layer, n, bn, out_dtype):
    bm = BM_IN_WIDE
    return pl.pallas_call(
        _mm_kernel,
        grid=(n // bn, ROWS // bm),
        in_specs=[pl.BlockSpec((bm, D), lambda j, i: (i, 0)), _w_spec(layer, bn)],
        out_specs=pl.BlockSpec((bm, bn), lambda j, i: (i, j)),
        out_shape=jax.ShapeDtypeStruct((ROWS, n), out_dtype),
        scratch_shapes=[pltpu.VMEM((D, bn), BF16)],
        name="in_proj",
        compiler_params=_params(("arbitrary", "arbitrary")),
    )(h, w)


def _swiglu_kernel(h_ref, wa_ref, wb_ref, o_ref, wa_scr, wb_scr):
    _cast_weights_once([(wa_ref, wa_scr), (wb_ref, wb_scr)])
    h = h_ref[...]
    a = _dot(h, wa_scr[...])
    b = _dot(h, wb_scr[...])
    o_ref[...] = (_silu(a) * b).astype(o_ref.dtype)


def _ffn_in(h, w, layer):
    bm, bn = BM_IN, 512
    nb = D_FF // bn
    return pl.pallas_call(
        _swiglu_kernel,
        grid=(nb, ROWS // bm),
        in_specs=[pl.BlockSpec((bm, D), lambda j, i: (i, 0)), _w_spec(layer, bn), _w_spec(layer, bn, nb)],
        out_specs=pl.BlockSpec((bm, bn), lambda j, i: (i, j)),
        out_shape=jax.ShapeDtypeStruct((ROWS, D_FF), BF16),
        scratch_shapes=[pltpu.VMEM((D, bn), BF16), pltpu.VMEM((D, bn), BF16)],
        name="ffn_in",
        compiler_params=_params(("arbitrary", "arbitrary")),
    )(h, w, w)


def _glu_kernel(h_ref, wa_ref, wb_ref, ba_ref, bb_ref, o_ref, wa_scr, wb_scr):
    _cast_weights_once([(wa_ref, wa_scr), (wb_ref, wb_scr)])
    h = h_ref[...]
    a = _dot(h, wa_scr[...]) + ba_ref[...]
    gt = _dot(h, wb_scr[...]) + bb_ref[...]
    o_ref[...] = a * (1.0 / (1.0 + jnp.exp(-gt)))


def _pw1_glu(h, w, b, layer):
    bm, bn = BM_IN, 512
    nb = D // bn
    b2 = b.reshape(1, 2 * D)
    return pl.pallas_call(
        _glu_kernel,
        grid=(nb, ROWS // bm),
        in_specs=[
            pl.BlockSpec((bm, D), lambda j, i: (i, 0)),
            _w_spec(layer, bn),
            _w_spec(layer, bn, nb),
            pl.BlockSpec((1, bn), lambda j, i: (0, j)),
            pl.BlockSpec((1, bn), lambda j, i: (0, j + nb)),
        ],
        out_specs=pl.BlockSpec((bm, bn), lambda j, i: (i, j)),
        out_shape=jax.ShapeDtypeStruct((ROWS, D), F32),
        scratch_shapes=[pltpu.VMEM((D, bn), BF16), pltpu.VMEM((D, bn), BF16)],
        name="pw1_glu",
        compiler_params=_params(("arbitrary", "arbitrary")),
    )(h, w, w, b2, b2)


def _gelu_tanh(x):
    return 0.5 * x * (1.0 + jnp.tanh(0.7978845608028654 * (x + 0.044715 * (x * x * x))))


def _sg_kernel(h_ref, wu_ref, wv_ref, g_ref, b_ref, wsp_ref, bsp_ref, o_ref):
    h = h_ref[...]
    u = _gelu_tanh(_dot(h, wu_ref[...]))
    vv = _gelu_tanh(_dot(h, wv_ref[...]))
    vv = _layer_norm(vv, g_ref[0], b_ref[0]).astype(BF16)
    wsp = wsp_ref[0]
    bsp = bsp_ref[0]
    for c in range(h.shape[0] // SG_CHUNK):
        rows = slice(c * SG_CHUNK, (c + 1) * SG_CHUNK)
        mixed = _dot(wsp, vv[rows]) + bsp
        o_ref[rows, :] = (u[rows] * mixed).astype(o_ref.dtype)


def _spatial_gating(h, w_sg, sg_g, sg_b, w_sp, b_sp):
    bm = BM_IN_WIDE
    return pl.pallas_call(
        _sg_kernel,
        grid=(ROWS // bm, SG_GROUPS),
        in_specs=[
            pl.BlockSpec((bm, D), lambda i, g: (i, 0)),
            pl.BlockSpec((D, SG_CH), lambda i, g: (0, g)),
            pl.BlockSpec((D, SG_CH), lambda i, g: (0, g + SG_GROUPS)),
            pl.BlockSpec((1, 1, SG_CH), lambda i, g: (g, 0, 0)),
            pl.BlockSpec((1, 1, SG_CH), lambda i, g: (g, 0, 0)),
            pl.BlockSpec((1, SG_CHUNK, SG_CHUNK), lambda i, g: (g, 0, 0)),
            pl.BlockSpec((1, SG_CHUNK, 1), lambda i, g: (g, 0, 0)),
        ],
        out_specs=pl.BlockSpec((bm, SG_CH), lambda i, g: (i, g)),
        out_shape=jax.ShapeDtypeStruct((ROWS, SG_W), BF16),
        name="spatial_gating",
        compiler_params=_params(("arbitrary", "arbitrary")),
    )(h, w_sg, w_sg, sg_g.reshape(SG_GROUPS, 1, SG_CH), sg_b.reshape(SG_GROUPS, 1, SG_CH),
      w_sp.astype(BF16), b_sp.reshape(SG_GROUPS, SG_CHUNK, 1))


def _chunk_triangles():
    i = jnp.arange(GLA_ROWS)[:, None]
    j = jnp.arange(GLA_ROWS)[None, :]
    same = (i // CHUNK) == (j // CHUNK)
    return jnp.stack([same & (j <= i), same & (j >= i)]).astype(BF16)


def _gla_kernel(*refs, seq, group, has_init, emit_state):
    q_ref, k_ref, v_ref, r_ref, g_ref, wg_ref, bg_ref, gg_ref, tri_ref = refs[:9]
    pos = 9
    s0_ref = None
    if has_init:
        s0_ref = refs[pos]
        pos += 1
    o_ref = refs[pos]
    pos += 1
    sfin_ref = None
    if emit_state:
        sfin_ref = refs[pos]
        pos += 1
    b_scr, of_scr, ob_scr = refs[pos:pos + 3]

    n_steps = seq // GLA_ROWS
    n_sub = GLA_ROWS // CHUNK
    scale = DK ** -0.5

    def decay_step(s, carry):
        rows = pl.ds(pl.multiple_of(s * GLA_ROWS, GLA_ROWS), GLA_ROWS)
        logits = _dot(g_ref[rows, :].astype(BF16), wg_ref[0]) + bg_ref[0]
        la = (jnp.minimum(logits, 0.0) - jnp.log1p(jnp.exp(-jnp.abs(logits)))) / TAU
        hi = la.astype(BF16)
        lo = (la - hi.astype(F32)).astype(BF16)
        tri = tri_ref[0]
        pre = _dot(tri, hi) + _dot(tri, lo)
        b_scr[rows, 0:DK] = pre[:, 0:DK]
        for c in range(n_sub):
            cs = slice(c * CHUNK, (c + 1) * CHUNK)
            total = pre[(c + 1) * CHUNK - 1:(c + 1) * CHUNK, DK:]
            b_scr[pl.ds(s * GLA_ROWS + c * CHUNK, CHUNK), DK:2 * DK] = total - pre[cs, DK:] + la[cs, DK:]
        return carry

    unroll = 2 if n_steps % 2 == 0 else 1
    all_steps = group * n_steps
    unroll_all = 2 if all_steps % 2 == 0 else 1
    lax.fori_loop(0, all_steps, decay_step, 0, unroll=unroll_all)

    def rows_step(r0, direction, st, out_scr):
        rows = pl.ds(r0, GLA_ROWS)
        b = b_scr[rows, direction * DK:(direction + 1) * DK]
        last = [(c + 1) * CHUNK - 1 if direction == 0 else c * CHUNK for c in range(n_sub)]
        rem = jnp.concatenate(
            [b[last[c]:last[c] + 1, :] - b[c * CHUNK:(c + 1) * CHUNK, :] for c in range(n_sub)], axis=0)
        q = q_ref[rows, :] * scale
        k = k_ref[rows, :]
        v = v_ref[rows, :].astype(BF16)
        q_in = (q * jnp.exp(b)).astype(BF16)
        k_in = (k * jnp.exp(-b)).astype(BF16)
        k_out = (k * jnp.exp(rem)).astype(BF16)
        mask = tri_ref[direction] > 0
        att = jnp.where(mask, _dot_nt(q_in, k_in), 0.0).astype(BF16)
        o_intra = _dot(att, v)
        order = range(n_sub) if direction == 0 else range(n_sub - 1, -1, -1)
        for c in order:
            cs = slice(c * CHUNK, (c + 1) * CHUNK)
            out_scr[pl.ds(r0 + c * CHUNK, CHUNK), :] = o_intra[cs] + _dot_nt(q_in[cs], st.astype(BF16))
            st = st * jnp.exp(b[last[c]:last[c] + 1, :]) + _dot_tn(v[cs], k_out[cs])
        return st

    for sq in range(group):
        base = sq * seq

        def body(s, carry, base=base):
            st_f, st_b = carry
            rf = pl.multiple_of(base + s * GLA_ROWS, GLA_ROWS)
            rb = pl.multiple_of(base + (n_steps - 1 - s) * GLA_ROWS, GLA_ROWS)
            st_f = rows_step(rf, 0, st_f, of_scr)
            st_b = rows_step(rb, 1, st_b, ob_scr)
            return st_f, st_b

        if has_init:
            init = (s0_ref[0].T, s0_ref[1].T)
        else:
            init = (jnp.zeros((DV, DK), F32), jnp.zeros((DV, DK), F32))
        if n_steps == 1:
            st_f, st_b = body(0, init)
        else:
            st_f, st_b = lax.fori_loop(0, n_steps, body, init, unroll=unroll)

        if emit_state:
            sfin_ref[sq, 0] = st_f.T
            sfin_ref[sq, 1] = st_b.T

    gg = gg_ref[0]

    def finish(i, carry):
        rows = pl.ds(pl.multiple_of(i * GLA_ROWS, GLA_ROWS), GLA_ROWS)
        o = _rms(of_scr[rows, :] + ob_scr[rows, :], gg)
        o_ref[rows, :] = (o * _silu(r_ref[rows, :])).astype(o_ref.dtype)
        return carry

    lax.fori_loop(0, all_steps, finish, 0, unroll=unroll_all)


def _gla(z1, wg, bg, gla_g, tri, seq, row_block0, n_seq, s0=None, layer_idx=0, emit_state=False, group=1):
    assert s0 is None or group == 1
    rows = seq * group
    qk_blocks = QK_W // DK
    v_block0 = 2 * QK_W // DV
    r_block0 = v_block0 + HEADS
    g_block = (2 * QK_W + 2 * V_W) // DK
    rb = row_block0
    in_specs = [
        pl.BlockSpec((rows, DK), lambda b, h: (rb + b, h)),
        pl.BlockSpec((rows, DK), lambda b, h: (rb + b, qk_blocks + h)),
        pl.BlockSpec((rows, DV), lambda b, h: (rb + b, v_block0 + h)),
        pl.BlockSpec((rows, DV), lambda b, h: (rb + b, r_block0 + h)),
        pl.BlockSpec((rows, DK), lambda b, h: (rb + b, g_block)),
        pl.BlockSpec((1, DK, 2 * DK), lambda b, h: (h, 0, 0)),
        pl.BlockSpec((1, 1, 2 * DK), lambda b, h: (h, 0, 0)),
        pl.BlockSpec((1, 1, DV), lambda b, h: (h, 0, 0)),
        pl.BlockSpec((2, GLA_ROWS, GLA_ROWS), lambda b, h: (0, 0, 0)),
    ]
    args = [z1, z1, z1, z1, z1, wg, bg, gla_g.reshape(HEADS, 1, DV), tri]
    if s0 is not None:
        in_specs.append(pl.BlockSpec((None, None, 2, None, DK, DV),
                                     lambda b, h: (b, layer_idx, 0, h, 0, 0)))
        args.append(s0)
    out_specs = [pl.BlockSpec((rows, DV), lambda b, h: (b, h))]
    out_shape = [jax.ShapeDtypeStruct((n_seq * seq, V_W), BF16)]
    if emit_state:
        out_specs.append(pl.BlockSpec((group, 2, None, DK, DV), lambda b, h: (b, 0, h, 0, 0)))
        out_shape.append(jax.ShapeDtypeStruct((n_seq, 2, HEADS, DK, DV), F32))
    kern = functools.partial(_gla_kernel, seq=seq, group=group, has_init=s0 is not None,
                             emit_state=emit_state)
    return pl.pallas_call(
        kern,
        grid=(n_seq // group, HEADS),
        in_specs=in_specs,
        out_specs=out_specs,
        out_shape=out_shape,
        scratch_shapes=[
            pltpu.VMEM((rows, 2 * DK), F32),
            pltpu.VMEM((rows, DV), F32),
            pltpu.VMEM((rows, DV), F32),
        ],
        name="gla_seq%d" % seq,
        compiler_params=_params(("arbitrary", "arbitrary")),
    )(*args)


def _ln_silu_rows(src_ref, g_ref, b_ref, o_ref):
    def step(r, carry):
        rows = pl.ds(pl.multiple_of(r * EPI_ROWS, EPI_ROWS), EPI_ROWS)
        y = _layer_norm(src_ref[rows, :], g_ref[...], b_ref[...])
        o_ref[rows, :] = _silu(y).astype(o_ref.dtype)
        return carry

    lax.fori_loop(0, o_ref.shape[0] // EPI_ROWS, step, 0, unroll=2)


def _conv_rows_kernel(z_ref, w_ref, b_ref, lg_ref, lb_ref, o_ref, pad_scr, conv_scr, *, seg):
    rows = z_ref.shape[0]
    n_seg = rows // seg
    plen = seg + 2 * CONV_HALO
    zeros = jnp.zeros((CONV_HALO, LANES), F32)
    first = CONV_HALO - HALF_K
    for ct in range(D // LANES):
        cols = slice(ct * LANES, (ct + 1) * LANES)
        for s in range(n_seg):
            base = s * plen
            pad_scr[ct, base:base + CONV_HALO, :] = zeros
            pad_scr[ct, base + CONV_HALO:base + CONV_HALO + seg, :] = z_ref[s * seg:(s + 1) * seg, cols]
            pad_scr[ct, base + CONV_HALO + seg:base + plen, :] = zeros
        acc = [jnp.zeros((seg, LANES), F32) + b_ref[:, cols] for _ in range(n_seg)]
        for t in range(CONV_K):
            tap = w_ref[t:t + 1, cols]
            for s in range(n_seg):
                start = s * plen + first + t
                acc[s] = acc[s] + pad_scr[ct, start:start + seg, :] * tap
        for s in range(n_seg):
            conv_scr[s * seg:(s + 1) * seg, cols] = acc[s]

    _ln_silu_rows(conv_scr, lg_ref, lb_ref, o_ref)


def _conv_rows(z, w, b, ln_g, ln_b, row0, n_rows, seg):
    rows = CONV_ROWS
    rb0 = row0 // rows
    plen = seg + 2 * CONV_HALO
    return pl.pallas_call(
        functools.partial(_conv_rows_kernel, seg=seg),
        grid=(n_rows // rows,),
        in_specs=[
            pl.BlockSpec((rows, D), lambda i: (rb0 + i, 0)),
            pl.BlockSpec((CONV_K, D), lambda i: (0, 0)),
            _row_vec_spec(),
            _row_vec_spec(),
            _row_vec_spec(),
        ],
        out_specs=pl.BlockSpec((rows, D), lambda i: (i, 0)),
        out_shape=jax.ShapeDtypeStruct((n_rows, D), BF16),
        scratch_shapes=[
            pltpu.VMEM((D // LANES, (rows // seg) * plen, LANES), F32),
            pltpu.VMEM((rows, D), F32),
        ],
        name="conv_rows_seg%d" % seg,
        compiler_params=_params(("arbitrary",)),
    )(z, w, b.reshape(1, D), ln_g.reshape(1, D), ln_b.reshape(1, D))


def _conv_cols_kernel(z_ref, w_ref, b_ref, o_ref, pad_scr):
    rows, ct = z_ref.shape
    halo = HALF_K * GRID_W
    zeros = jnp.zeros((halo, ct), F32)
    pad_scr[0:halo, :] = zeros
    pad_scr[halo:halo + rows, :] = z_ref[...]
    pad_scr[halo + rows:2 * halo + rows, :] = zeros
    blk = 256

    def body(i, carry):
        r0 = pl.multiple_of(i * blk, blk)
        acc = jnp.zeros((blk, ct), F32) + b_ref[...]
        for t in range(CONV_K):
            acc = acc + pad_scr[pl.ds(r0 + t * GRID_W, blk), :] * w_ref[t:t + 1, :]
        o_ref[pl.ds(r0, blk), :] = acc
        return carry

    lax.fori_loop(0, rows // blk, body, 0)


def _conv_cols(z, w, b, row0, n_rows, ct):
    rows = DEC_SEQ
    rb0 = row0 // rows
    return pl.pallas_call(
        _conv_cols_kernel,
        grid=(n_rows // rows, D // ct),
        in_specs=[
            pl.BlockSpec((rows, ct), lambda i, j: (rb0 + i, j)),
            pl.BlockSpec((CONV_K, ct), lambda i, j: (0, j)),
            pl.BlockSpec((1, ct), lambda i, j: (0, j)),
        ],
        out_specs=pl.BlockSpec((rows, ct), lambda i, j: (i, j)),
        out_shape=jax.ShapeDtypeStruct((n_rows, D), F32),
        scratch_shapes=[pltpu.VMEM((rows + 2 * HALF_K * GRID_W, ct), F32)],
        name="conv_cols",
        compiler_params=_params(("arbitrary", "arbitrary")),
    )(z, w, b.reshape(1, D))


def _ln_silu_kernel(y_ref, g_ref, b_ref, o_ref):
    _ln_silu_rows(y_ref, g_ref, b_ref, o_ref)


def _ln_silu(y, ln_g, ln_b):
    bm = 512
    n_rows = y.shape[0]
    return pl.pallas_call(
        _ln_silu_kernel,
        grid=(n_rows // bm,),
        in_specs=[pl.BlockSpec((bm, D), lambda i: (i, 0)), _row_vec_spec(), _row_vec_spec()],
        out_specs=pl.BlockSpec((bm, D), lambda i: (i, 0)),
        out_shape=jax.ShapeDtypeStruct((n_rows, D), BF16),
        name="ln_silu",
        compiler_params=_params(("arbitrary",)),
    )(y, ln_g.reshape(1, D), ln_b.reshape(1, D))


def _out_kernel(*refs, terms, nj, bn, p_tiles, has_bias, emit_h, split_out):
    pos = 0
    lhs = []
    for split in terms:
        n = 2 if split else 1
        lhs.append(refs[pos:pos + n])
        pos += n
    w_refs = refs[pos:pos + len(terms)]
    pos += len(terms)
    if has_bias:
        bias_ref = refs[pos]
        pos += 1
    x_ref, ga_ref, gate_ref = refs[pos:pos + 3]
    pos += 3
    if emit_h:
        gb_ref, sh_ref, sc_ref = refs[pos:pos + 3]
        pos += 3
    n_out = 2 if split_out else 1
    xo_refs = refs[pos:pos + n_out]
    pos += n_out
    if emit_h:
        h_ref = refs[pos]
        pos += 1
    y_scr, vec_scr = refs[pos:pos + 2]

    i = pl.program_id(0)
    j = pl.program_id(1)

    def column_tile(latent):
        y = None
        for l, w in zip(lhs, w_refs):
            a = l[1 if (latent and len(l) == 2) else 0][...]
            d = _dot(a, w[...])
            y = d if y is None else y + d
        y_scr[j] = y

    if any(terms):
        pl.when(i < p_tiles)(lambda: column_tile(False))
        pl.when(i >= p_tiles)(lambda: column_tile(True))
    else:
        column_tile(False)

    col_slices = [slice(jj * bn, (jj + 1) * bn) for jj in range(nj)]

    def finish(xo_ref):
        vec_scr[0] = gate_ref[0] * ga_ref[...]
        if emit_h:
            vec_scr[1] = gb_ref[...] * (1.0 + sc_ref[0])

        def rows_step(r, carry):
            rows = pl.ds(pl.multiple_of(r * EPI_ROWS, EPI_ROWS), EPI_ROWS)
            ys = []
            ssq = None
            for jj, cols in enumerate(col_slices):
                y = y_scr[jj, rows, :]
                if has_bias:
                    y = y + bias_ref[:, cols]
                ys.append(y)
                part = jnp.sum(y * y, axis=-1, keepdims=True)
                ssq = part if ssq is None else ssq + part
            inv = lax.rsqrt(ssq * (1.0 / D) + EPS)
            xs = []
            ssq = None
            for jj, cols in enumerate(col_slices):
                x_new = x_ref[rows, cols] + ys[jj] * inv * vec_scr[0, :, cols]
                xo_ref[rows, cols] = x_new
                if emit_h:
                    xs.append(x_new)
                    part = jnp.sum(x_new * x_new, axis=-1, keepdims=True)
                    ssq = part if ssq is None else ssq + part
            if emit_h:
                inv = lax.rsqrt(ssq * (1.0 / D) + EPS)
                for jj, cols in enumerate(col_slices):
                    h = xs[jj] * inv * vec_scr[1, :, cols] + sh_ref[0, :, cols]
                    h_ref[rows, cols] = h.astype(BF16)
            return carry

        lax.fori_loop(0, xo_ref.shape[0] // EPI_ROWS, rows_step, 0, unroll=2)

    last = j == nj - 1
    if split_out:
        pl.when(last & (i < p_tiles))(lambda: finish(xo_refs[0]))
        pl.when(last & (i >= p_tiles))(lambda: finish(xo_refs[1]))
    else:
        pl.when(last)(lambda: finish(xo_refs[0]))


def _out_proj(terms, w, layer_w, x, mod, g_y, gate_idx, next_norm, *, bn, bias=None, split_out=False):
    bm = BM_OUT
    nj = D // bn
    p_tiles = P_ROWS // bm
    s_tiles = S_ROWS // bm
    emit_h = next_norm is not None
    in_specs, args = [], []
    for arrays, _, kt in terms:
        if len(arrays) == 2:
            in_specs += [
                pl.BlockSpec((bm, kt), lambda i, j: (jnp.minimum(i, p_tiles - 1), 0)),
                pl.BlockSpec((bm, kt), lambda i, j: (jnp.clip(i - p_tiles, 0, s_tiles - 1), 0)),
            ]
        else:
            in_specs.append(pl.BlockSpec((bm, kt), lambda i, j: (i, 0)))
        args += list(arrays)
    for _, k0, kt in terms:
        kb = k0 // kt
        if w.ndim == 4:
            in_specs.append(pl.BlockSpec((None, None, kt, bn), lambda i, j, kb=kb: (layer_w, j, kb, 0)))
        else:
            in_specs.append(pl.BlockSpec((None, kt, bn), lambda i, j, kb=kb: (layer_w, kb, j)))
        args.append(w)
    if bias is not None:
        in_specs.append(_row_vec_spec())
        args.append(bias.reshape(1, D))
    in_specs += [pl.BlockSpec((bm, D), lambda i, j: (i, 0)),
                 _row_vec_spec(), _mod_spec(gate_idx[0], gate_idx[1], bm)]
    args += [x, g_y.reshape(1, D), mod]
    if split_out:
        out_specs = [pl.BlockSpec((bm, D), lambda i, j: (jnp.minimum(i, p_tiles - 1), 0)),
                     pl.BlockSpec((bm, D), lambda i, j: (jnp.clip(i - p_tiles, 0, s_tiles - 1), 0))]
        out_shape = [jax.ShapeDtypeStruct((P_ROWS, D), F32), jax.ShapeDtypeStruct((S_ROWS, D), F32)]
    else:
        out_specs = [pl.BlockSpec((bm, D), lambda i, j: (i, 0))]
        out_shape = [jax.ShapeDtypeStruct((ROWS, D), F32)]
    if emit_h:
        gain, (layer, shift_chunk, scale_chunk) = next_norm
        in_specs += [_row_vec_spec(), _mod_spec(layer, shift_chunk, bm), _mod_spec(layer, scale_chunk, bm)]
        args += [gain.reshape(1, D), mod, mod]
        out_specs.append(pl.BlockSpec((bm, D), lambda i, j: (i, 0)))
        out_shape.append(jax.ShapeDtypeStruct((ROWS, D), BF16))
    kern = functools.partial(_out_kernel, terms=tuple(len(t[0]) == 2 for t in terms), nj=nj, bn=bn,
                             p_tiles=p_tiles, has_bias=bias is not None, emit_h=emit_h, split_out=split_out)
    res = pl.pallas_call(
        kern,
        grid=(ROWS // bm, nj),
        in_specs=in_specs,
        out_specs=out_specs,
        out_shape=out_shape,
        scratch_shapes=[pltpu.VMEM((nj, bm, bn), F32), pltpu.VMEM((2, 1, D), F32)],
        name="out_proj_k%d" % sum(t[2] for t in terms),
        compiler_params=_params(("arbitrary", "arbitrary")),
    )(*args)
    if split_out:
        return (res[0], res[1]), None
    return (res[0], res[1]) if emit_h else (res[0], None)


def _out_lagged_kernel(*refs, terms, n_tiles, p_tiles, has_bias):
    pos = 0
    lhs = []
    for split in terms:
        n = 2 if split else 1
        lhs.append(refs[pos:pos + n])
        pos += n
    w_refs = refs[pos:pos + len(terms)]
    pos += len(terms)
    if has_bias:
        bias_ref = refs[pos]
        pos += 1
    x_ref, ga_ref, gate_ref, gb_ref, sh_ref, sc_ref, xo_ref, h_ref, y_even, y_odd, vec_scr = refs[pos:pos + 11]

    i = pl.program_id(0)
    bm = xo_ref.shape[0]

    @pl.when(i == 0)
    def _():
        y_odd[...] = jnp.zeros_like(y_odd)

    def step(y_write, y_read):
        vec_scr[0] = gate_ref[0] * ga_ref[...]
        vec_scr[1] = gb_ref[...] * (1.0 + sc_ref[0])
        for r in range(bm // LAG_ROWS):
            rows = slice(r * LAG_ROWS, (r + 1) * LAG_ROWS)
            yr = y_read[rows, :]
            if has_bias:
                yr = yr + bias_ref[...]
            inv = lax.rsqrt(jnp.sum(yr * yr, axis=-1, keepdims=True) * (1.0 / D) + EPS)
            x_new = x_ref[rows, :] + yr * inv * vec_scr[0]
            xo_ref[rows, :] = x_new
            inv = lax.rsqrt(jnp.sum(x_new * x_new, axis=-1, keepdims=True) * (1.0 / D) + EPS)
            h_ref[rows, :] = (x_new * inv * vec_scr[1] + sh_ref[0]).astype(BF16)

        y = None
        for l, w in zip(lhs, w_refs):
            a = jnp.where(i < p_tiles, l[0][...], l[1][...]) if len(l) == 2 else l[0][...]
            d = _dot(a, w[...])
            y = d if y is None else y + d
        y_write[...] = y

    pl.when(i % 2 == 0)(lambda: step(y_even, y_odd))
    pl.when(i % 2 == 1)(lambda: step(y_odd, y_even))


def _out_proj_lagged(terms, w, layer_w, x, mod, g_y, gate_idx, next_norm, *, bias=None):
    bm = BM_OUT
    n_tiles = ROWS // bm
    p_tiles = P_ROWS // bm
    s_tiles = S_ROWS // bm
    in_specs, args = [], []
    for arrays, _, kt in terms:
        if len(arrays) == 2:
            in_specs += [
                pl.BlockSpec((bm, kt), lambda i: (jnp.minimum(i, p_tiles - 1), 0)),
                pl.BlockSpec((bm, kt), lambda i: (jnp.clip(i - p_tiles, 0, s_tiles - 1), 0)),
            ]
        else:
            in_specs.append(pl.BlockSpec((bm, kt), lambda i: (jnp.minimum(i, n_tiles - 1), 0)))
        args += list(arrays)
    for _, k0, kt in terms:
        in_specs.append(pl.BlockSpec((None, kt, D), lambda i, kb=k0 // kt: (layer_w, kb, 0)))
        args.append(w)
    if bias is not None:
        in_specs.append(_row_vec_spec())
        args.append(bias.reshape(1, D))
    prev = lambda i: (jnp.maximum(i - 1, 0), 0)
    gain, (layer, shift_chunk, scale_chunk) = next_norm
    in_specs += [pl.BlockSpec((bm, D), prev), _row_vec_spec(), _mod_spec(gate_idx[0], gate_idx[1], bm, lag=1),
                 _row_vec_spec(), _mod_spec(layer, shift_chunk, bm, lag=1), _mod_spec(layer, scale_chunk, bm, lag=1)]
    args += [x, g_y.reshape(1, D), mod, gain.reshape(1, D), mod, mod]
    kern = functools.partial(_out_lagged_kernel, terms=tuple(len(t[0]) == 2 for t in terms),
                             n_tiles=n_tiles, p_tiles=p_tiles, has_bias=bias is not None)
    x_new, h = pl.pallas_call(
        kern,
        grid=(n_tiles + 1,),
        in_specs=in_specs,
        out_specs=[pl.BlockSpec((bm, D), prev), pl.BlockSpec((bm, D), prev)],
        out_shape=[jax.ShapeDtypeStruct((ROWS, D), F32), jax.ShapeDtypeStruct((ROWS, D), BF16)],
        scratch_shapes=[pltpu.VMEM((bm, D), F32), pltpu.VMEM((bm, D), F32), pltpu.VMEM((2, 1, D), F32)],
        name="out_proj_lagged_k%d" % sum(t[2] for t in terms),
        compiler_params=_params(("arbitrary",)),
    )(*args)
    return x_new, h


def kernel(x_prompt, x_sample, state_gla, c, c_ctx, norm_g, w_mod, b_mod, w_in_a, w_gate_up, b_gate, gla_norm_g, sg_norm_g, sg_norm_b, w_spatial, b_spatial, w_out_a, w_pw1, b_pw1, w_dw, b_dw, conv_norm_g, conv_norm_b, w_pw2, b_pw2, w_ffn_in, w_ffn_out):
    cv = jnp.concatenate([c_ctx[None, :], c, jnp.zeros((MOD_ROWS - 1 - DEC_BATCH, D), F32)], axis=0)
    mod = _modulation(cv, w_mod, b_mod).reshape(DEPTH * MOD_ROWS * N_MOD, 1, D)
    tri = _chunk_triangles()
    n_z1 = 2 * QK_W + 2 * V_W + 2 * RANK
    w_z1 = jnp.pad(w_in_a[:, :, :n_z1], ((0, 0), (0, 0), (0, Z1_W - n_z1))).astype(BF16)
    n_col = D // FFN_OUT_BN
    w_out_bf = w_out_a.astype(BF16)
    w_pw2_bf = w_pw2.astype(BF16)
    w_ffn_out_bf = w_ffn_out.astype(BF16).reshape(DEPTH, D_FF, n_col, FFN_OUT_BN).transpose(0, 2, 1, 3)

    x, h = _first_norm(x_prompt.reshape(P_ROWS, D), x_sample.reshape(S_ROWS, D),
                       norm_g[0, 0].reshape(1, D), mod)
    new_states = []
    for layer in range(DEPTH):
        g = norm_g[layer]
        i = layer // 2
        mid_norm = (g[2], (layer, 3, 4))
        if layer % 2 == 0:
            w_sg = w_in_a[i, :, n_z1:].astype(BF16)
            z1 = _matmul(h, w_z1, i, Z1_W, 640, F32)
            sg_out = _spatial_gating(h, w_sg, sg_norm_g[i], sg_norm_b[i], w_spatial[i], b_spatial[i])

            wf = w_gate_up[i, 0].reshape(RANK, HEADS, DK).transpose(1, 0, 2)
            wb = w_gate_up[i, 1].reshape(RANK, HEADS, DK).transpose(1, 0, 2)
            wg = jnp.zeros((HEADS, DK, 2 * DK), F32)
            wg = wg.at[:, 0:RANK, 0:DK].set(wf).at[:, RANK:2 * RANK, DK:2 * DK].set(wb).astype(BF16)
            bg = jnp.concatenate([b_gate[i, 0].reshape(HEADS, 1, DK), b_gate[i, 1].reshape(HEADS, 1, DK)],
                                 axis=-1)
            o_p, s_new = _gla(z1, wg, bg, gla_norm_g[i], tri, SEQ, 0, BATCH, emit_state=True, group=4)
            (o_s,) = _gla(z1, wg, bg, gla_norm_g[i], tri, DEC_SEQ, P_ROWS // DEC_SEQ, DEC_BATCH,
                          s0=state_gla, layer_idx=i)
            new_states.append(s_new)

            x, h = _out_proj_lagged([((o_p, o_s), 0, V_W), ((sg_out,), V_W, SG_W)], w_out_bf, i,
                                    x, mod, g[1], (layer, 2), mid_norm)
        else:
            z = _pw1_glu(h, w_pw1, b_pw1[i], i)
            ln = (conv_norm_g[i], conv_norm_b[i])
            conv_p = _conv_rows(z, w_dw[i], b_dw[i], *ln, 0, P_ROWS, SEQ)
            if i % 2 == 1:
                conv_s = _ln_silu(_conv_cols(z, w_dw[i], b_dw[i], P_ROWS, S_ROWS, 128), *ln)
            else:
                conv_s = _conv_rows(z, w_dw[i], b_dw[i], *ln, P_ROWS, S_ROWS, GRID_W)
            x, h = _out_proj_lagged([((conv_p, conv_s), 0, D)], w_pw2_bf, i,
                                    x, mod, g[1], (layer, 2), mid_norm, bias=b_pw2[i])

        act = _ffn_in(h, w_ffn_in, layer)
        final = layer == DEPTH - 1
        next_norm = None if final else (norm_g[layer + 1, 0], (layer + 1, 0, 1))
        x, h = _out_proj([((act,), 0, D_FF)], w_ffn_out_bf, layer,
                         x, mod, g[3], (layer, 5), next_norm, bn=FFN_OUT_BN, split_out=final)

    y_prompt = x[0].reshape(BATCH, SEQ, D)
    y_sample = x[1].reshape(DEC_BATCH, DEC_SEQ, D)
    state_new = jnp.stack(new_states, axis=1).astype(x_prompt.dtype)
    return (y_prompt, y_sample, state_new)
```
